```python
import math
import jax, jax.numpy as jnp
from jax import lax
import numpy as np

D_MODEL = 4096
BATCH = 1
SEQ = 8192
DEPTH = 1

CHUNK = 64
Q_BLOCK = 128
HEAD_DIM = 128
ATTN_WIDTH = D_MODEL // 2
ATTN_HEADS = ATTN_WIDTH // HEAD_DIM
RNN_WIDTH = D_MODEL // 2
RNN_BLOCKS = 16
RNN_BLOCK_DIM = RNN_WIDTH // RNN_BLOCKS
CONV_W = 4
RG_C = 8.0
N_BRANCH = 2
EPS = 1e-6
IN_SPLITS = (ATTN_WIDTH, ATTN_WIDTH, ATTN_WIDTH, ATTN_WIDTH, ATTN_HEADS, RNN_WIDTH, RNN_WIDTH)
IN_COLS = sum(IN_SPLITS)

kernel_name = "hybrid_fox_rglru_gated_merge"


def _rms(xf, eps=EPS):
    return xf * lax.rsqrt(jnp.mean(jnp.square(xf), axis=-1, keepdims=True) + eps)


def _split_cols(p):
    idx = np.cumsum(IN_SPLITS)[:-1].tolist()
    return jnp.split(p, idx, axis=-1)


def _forgetting_attention(q, k, v, log_f):
    B, H, S, Dh = q.shape
    n_blocks = S // Q_BLOCK
    scale = 1.0 / math.sqrt(Dh)
    F = jnp.cumsum(log_f, axis=-1)
    k_pos = jnp.arange(S)

    def block(i):
        start = i * Q_BLOCK
        q_blk = lax.dynamic_slice_in_dim(q, start, Q_BLOCK, axis=2)
        F_q = lax.dynamic_slice_in_dim(F, start, Q_BLOCK, axis=2)
        s = jnp.einsum('bhqd,bhkd->bhqk', q_blk, k) * scale
        s = s + (F_q[..., :, None] - F[..., None, :])
        q_pos = start + jnp.arange(Q_BLOCK)
        mask = q_pos[:, None] >= k_pos[None, :]
        s = jnp.where(mask, s, jnp.float32(-1e30))
        p = jax.nn.softmax(s, axis=-1)
        return jnp.einsum('bhqk,bhkd->bhqd', p, v)

    out = lax.map(block, jnp.arange(n_blocks))
    out = jnp.transpose(out, (1, 2, 0, 3, 4)).reshape(B, H, S, Dh)
    return out


def _causal_depthwise_conv(x, w, b):
    S = x.shape[1]
    xp = jnp.pad(x, ((0, 0), (CONV_W - 1, 0), (0, 0)))
    out = b
    for tap in range(CONV_W):
        out = out + xp[:, tap:tap + S, :] * w[tap]
    return out


def _rg_lru(xr, w_rg_a, b_rg_a, w_rg_x, b_rg_x, lru_lambda):
    B, S, W = xr.shape
    xb = xr.reshape(B, S, RNN_BLOCKS, RNN_BLOCK_DIM)
    r = jax.nn.sigmoid(jnp.einsum('bsnd,nde->bsne', xb, w_rg_a).reshape(B, S, W) + b_rg_a)
    i = jax.nn.sigmoid(jnp.einsum('bsnd,nde->bsne', xb, w_rg_x).reshape(B, S, W) + b_rg_x)
    log_a = -RG_C * r * jax.nn.softplus(-lru_lambda)
    a = jnp.exp(log_a)
    mult = jnp.sqrt(-jnp.expm1(2.0 * log_a))
    bx = mult * (i * xr)

    def combine(left, right):
        a_l, b_l = left
        a_r, b_r = right
        return a_l * a_r, a_r * b_l + b_r

    _, h = lax.associative_scan(combine, (a, bx), axis=1)
    return h


def setup_inputs(seed: int = 0) -> dict:
    key = jax.random.key(seed)
    ks = jax.random.split(key, 24)
    D = D_MODEL
    f32 = jnp.float32
    nrm = lambda k, shape, s: jax.random.normal(k, shape, f32) * s
    x = nrm(ks[0], (BATCH, SEQ, D), 1.0)
    c = nrm(ks[1], (BATCH, D), 1.0)
    w_ada = nrm(ks[2], (D, 3 * D), 0.5 * D ** -0.5)
    b_ada = nrm(ks[3], (3 * D,), 0.02)
    norm_w = 1.0 + nrm(ks[4], (D,), 0.02)
    w_in = nrm(ks[5], (D, IN_COLS), D ** -0.5)
    b_f = 3.0 + nrm(ks[6], (ATTN_HEADS,), 0.5)
    q_norm_w = 1.0 + nrm(ks[7], (HEAD_DIM,), 0.02)
    k_norm_w = 1.0 + nrm(ks[8], (HEAD_DIM,), 0.02)
    conv_w = nrm(ks[9], (CONV_W, RNN_WIDTH), CONV_W ** -0.5)
    conv_b = nrm(ks[10], (RNN_WIDTH,), 0.02)
    w_rg_a = nrm(ks[11], (RNN_BLOCKS, RNN_BLOCK_DIM, RNN_BLOCK_DIM), RNN_BLOCK_DIM ** -0.5)
    b_rg_a = nrm(ks[12], (RNN_WIDTH,), 0.02)
    w_rg_x = nrm(ks[13], (RNN_BLOCKS, RNN_BLOCK_DIM, RNN_BLOCK_DIM), RNN_BLOCK_DIM ** -0.5)
    b_rg_x = nrm(ks[14], (RNN_WIDTH,), 0.02)
    a_base = jax.random.uniform(ks[15], (RNN_WIDTH,), f32, 0.9, 0.999)
    sig_l = a_base ** (1.0 / RG_C)
    lru_lambda = jnp.log(sig_l) - jnp.log1p(-sig_l)
    w_br_attn = nrm(ks[16], (ATTN_WIDTH, D), ATTN_WIDTH ** -0.5)
    w_br_rnn = nrm(ks[17], (RNN_WIDTH, D), RNN_WIDTH ** -0.5)
    w_gate = nrm(ks[18], (D, N_BRANCH * D), D ** -0.5)
    b_gate = nrm(ks[19], (N_BRANCH * D,), 0.02)
    w_out = nrm(ks[20], (D, D), D ** -0.5)
    return {"x": x, "c": c, "w_ada": w_ada, "b_ada": b_ada, "norm_w": norm_w,
            "w_in": w_in, "b_f": b_f, "q_norm_w": q_norm_w, "k_norm_w": k_norm_w,
            "conv_w": conv_w, "conv_b": conv_b, "w_rg_a": w_rg_a, "b_rg_a": b_rg_a,
            "w_rg_x": w_rg_x, "b_rg_x": b_rg_x, "lru_lambda": lru_lambda,
            "w_br_attn": w_br_attn, "w_br_rnn": w_br_rnn, "w_gate": w_gate,
            "b_gate": b_gate, "w_out": w_out}


def reference(x, c, w_ada, b_ada, norm_w, w_in, b_f, q_norm_w, k_norm_w, conv_w, conv_b,
              w_rg_a, b_rg_a, w_rg_x, b_rg_x, lru_lambda, w_br_attn, w_br_rnn, w_gate,
              b_gate, w_out):
    dt = x.dtype
    f32 = jnp.float32
    B, S, D = x.shape
    mod = jnp.dot(jax.nn.silu(c.astype(f32)), w_ada.astype(f32)) + b_ada.astype(f32)
    shift, scale, gate_res = jnp.split(mod, 3, axis=-1)

    for _layer in range(DEPTH):
        h = _rms(x.astype(f32)) * norm_w.astype(f32)
        h = (h * (1.0 + scale[:, None, :]) + shift[:, None, :]).astype(dt)

        proj = jnp.einsum('bsd,de->bse', h, w_in)
        q, k, v, z_a, f_logit, x_r, z_r = _split_cols(proj)

        def heads(t):
            return t.reshape(B, S, ATTN_HEADS, HEAD_DIM).transpose(0, 2, 1, 3).astype(f32)
        qh = _rms(heads(q)) * q_norm_w.astype(f32)
        kh = _rms(heads(k)) * k_norm_w.astype(f32)
        vh = heads(v)
        log_f = jax.nn.log_sigmoid(f_logit.astype(f32) + b_f.astype(f32))
        log_f = jnp.transpose(log_f, (0, 2, 1))
        att = _forgetting_attention(qh, kh, vh, log_f)
        att = att.transpose(0, 2, 1, 3).reshape(B, S, ATTN_WIDTH)
        y_a = (att * jax.nn.silu(z_a.astype(f32))).astype(dt)
        y_a = jnp.einsum('bsw,wd->bsd', y_a, w_br_attn)

        xc = _causal_depthwise_conv(x_r.astype(f32), conv_w.astype(f32), conv_b.astype(f32))
        hr = _rg_lru(xc, w_rg_a.astype(f32), b_rg_a.astype(f32), w_rg_x.astype(f32),
                     b_rg_x.astype(f32), lru_lambda.astype(f32))
        y_r = (hr * jax.nn.silu(z_r.astype(f32))).astype(dt)
        y_r = jnp.einsum('bsw,wd->bsd', y_r, w_br_rnn)

        g = jax.nn.sigmoid(jnp.einsum('bsd,de->bse', h, w_gate).astype(f32) + b_gate.astype(f32))
        g_a, g_r = jnp.split(g, N_BRANCH, axis=-1)
        merged = (g_a * y_a.astype(f32) + g_r * y_r.astype(f32)).astype(dt)
        out = jnp.einsum('bsd,de->bse', merged, w_out)
        x = (x.astype(f32) + gate_res[:, None, :] * out.astype(f32)).astype(dt)
    return x
```

```python
import functools
import math

import jax
import jax.numpy as jnp
from jax import lax
from jax.experimental import pallas as pl
from jax.experimental.pallas import tpu as pltpu

F32 = jnp.float32
BF16 = jnp.bfloat16

LANES = 128
HEAD_DIM = 128
CONV_W = 4
RG_C = 8.0
EPS = 1e-6
MASK_VALUE = -1e30
MIB = 1024 * 1024


def _params(semantics, vmem_mib):
    return pltpu.CompilerParams(dimension_semantics=semantics,
                                vmem_limit_bytes=vmem_mib * MIB)


def _sigmoid(x):
    return 1.0 / (1.0 + jnp.exp(-x))


def _silu(x):
    return x * _sigmoid(x)


def _softplus(x):
    return jnp.maximum(x, 0.0) + jnp.log1p(jnp.exp(-jnp.abs(x)))


def _ada_kernel(c_ref, w_ref, b_ref, o_ref, *, rows):
    d = w_ref.shape[0]

    def body(r, acc):
        r0 = pl.multiple_of(r * rows, rows)
        c = c_ref[pl.ds(r0, rows), :]
        return acc + jnp.sum(w_ref[pl.ds(r0, rows), :] * _silu(c), axis=0, keepdims=True)

    acc = lax.fori_loop(0, d // rows, body, jnp.zeros(o_ref.shape, F32))
    o_ref[...] = acc + b_ref[...]


def _ada(c, w_ada, b_ada, *, tn=512, rows=512):
    d, n = w_ada.shape
    return pl.pallas_call(
        functools.partial(_ada_kernel, rows=rows),
        grid=(n // tn,),
        in_specs=[pl.BlockSpec((d, 1), lambda j: (0, 0)),
                  pl.BlockSpec((d, tn), lambda j: (0, j)),
                  pl.BlockSpec((1, tn), lambda j: (0, j))],
        out_specs=pl.BlockSpec((1, tn), lambda j: (0, j)),
        out_shape=jax.ShapeDtypeStruct((1, n), F32),
        compiler_params=_params(("parallel",), 40),
        name="ada",
    )(c.reshape(d, 1), w_ada, b_ada.reshape(1, n))


def _prenorm_kernel(x_ref, nw_ref, scale_ref, shift_ref, o_ref):
    x = x_ref[...]
    ms = jnp.mean(x * x, axis=-1, keepdims=True)
    h = x * lax.rsqrt(ms + EPS) * nw_ref[...]
    o_ref[...] = (h * (1.0 + scale_ref[...]) + shift_ref[...]).astype(o_ref.dtype)


def _prenorm(x, norm_w, scale, shift, *, tm=256):
    s, d = x.shape
    vec = pl.BlockSpec((1, d), lambda i: (0, 0))
    return pl.pallas_call(
        _prenorm_kernel,
        grid=(s // tm,),
        in_specs=[pl.BlockSpec((tm, d), lambda i: (i, 0)), vec, vec, vec],
        out_specs=pl.BlockSpec((tm, d), lambda i: (i, 0)),
        out_shape=jax.ShapeDtypeStruct((s, d), BF16),
        compiler_params=_params(("parallel",), 40),
        name="prenorm",
    )(x, norm_w.reshape(1, d), scale, shift)


def _proj_kernel(a_ref, w_ref, *rest, epilogue, silu_from):
    o_ref = rest[-1]
    y = jnp.dot(a_ref[...], w_ref[...], preferred_element_type=F32)
    if epilogue == "plain":
        o_ref[...] = y.astype(o_ref.dtype)
    elif epilogue == "silu":
        o_ref[...] = _silu(y).astype(o_ref.dtype)
    elif epilogue == "plain_then_silu":
        j = pl.program_id(1)

        @pl.when(j < silu_from)
        def _():
            o_ref[...] = y.astype(o_ref.dtype)

        @pl.when(j >= silu_from)
        def _():
            o_ref[...] = _silu(y).astype(o_ref.dtype)
    elif epilogue == "headnorm":
        gain_ref = rest[0]
        for hh in range(y.shape[1] // HEAD_DIM):
            sl = slice(hh * HEAD_DIM, (hh + 1) * HEAD_DIM)
            yh = y[:, sl]
            ms = jnp.mean(yh * yh, axis=-1, keepdims=True)
            o_ref[:, sl] = (yh * lax.rsqrt(ms + EPS) * gain_ref[:, sl]).astype(o_ref.dtype)
    elif epilogue == "logsigmoid":
        z = y + rest[0][...]
        o_ref[...] = (jnp.minimum(z, 0.0) - jnp.log1p(jnp.exp(-jnp.abs(z)))).astype(o_ref.dtype)
    elif epilogue == "residual":
        x_ref, g_ref = rest[0], rest[1]
        o_ref[...] = (x_ref[...] + g_ref[...] * y).astype(o_ref.dtype)
    else:
        raise ValueError(epilogue)


def _proj(a, w, *, out_dtype, epilogue, tm, tn, row_vecs=(), tiles=(), silu_from=0, name):
    m, k = a.shape
    n = w.shape[1]
    in_specs = [pl.BlockSpec((tm, k), lambda i, j: (i, 0)),
                pl.BlockSpec((k, tn), lambda i, j: (0, j))]
    extras = []
    for t in tiles:
        in_specs.append(pl.BlockSpec((tm, tn), lambda i, j: (i, j)))
        extras.append(t)
    for v in row_vecs:
        in_specs.append(pl.BlockSpec((1, tn), lambda i, j: (0, j)))
        extras.append(v.reshape(1, n))
    return pl.pallas_call(
        functools.partial(_proj_kernel, epilogue=epilogue, silu_from=silu_from),
        grid=(m // tm, n // tn),
        in_specs=in_specs,
        out_specs=pl.BlockSpec((tm, tn), lambda i, j: (i, j)),
        out_shape=jax.ShapeDtypeStruct((m, n), out_dtype),
        compiler_params=_params(("parallel", "arbitrary"), 52),
        name=name,
    )(a, w, *extras)


def _cumsum_kernel(x_ref, ft_ref, *, chunk):
    s = x_ref.shape[0]
    row = lax.broadcasted_iota(jnp.int32, (chunk, LANES), 0)

    def body(c, carry):
        s0 = pl.multiple_of(c * chunk, chunk)
        x = x_ref[pl.ds(s0, chunk), :]
        d = 1
        while d < chunk:
            x = x + jnp.where(row >= d, pltpu.roll(x, d, 0), 0.0)
            d *= 2
        x = x + carry
        ft_ref[:, pl.ds(s0, chunk)] = x.T
        return x[chunk - 1:chunk, :]

    lax.fori_loop(0, s // chunk, body, jnp.zeros((1, LANES), F32))


def _cumsum_t(logf, *, chunk=512):
    s = logf.shape[0]
    return pl.pallas_call(
        functools.partial(_cumsum_kernel, chunk=chunk),
        out_shape=jax.ShapeDtypeStruct((LANES, s), F32),
        compiler_params=_params(None, 40),
        name="cumsum",
    )(logf)


def _attn_kernel(q_ref, k_ref, v_ref, ft_ref, z_ref, o_ref, m_ref, l_ref, acc_ref, *, tq, tk):
    i = pl.program_id(1)
    q0 = pl.multiple_of(i * tq, tq)
    q = q_ref[...]
    reps = tk // LANES
    fq = jnp.broadcast_to(ft_ref[:, pl.ds(q0, tq)], (LANES, tq)).T
    fq = jnp.concatenate([fq] * reps, axis=1)

    m_ref[...] = jnp.full(m_ref.shape, MASK_VALUE, F32)
    l_ref[...] = jnp.zeros(l_ref.shape, F32)
    acc_ref[...] = jnp.zeros(acc_ref.shape, F32)

    def block(j, masked):
        k0 = pl.multiple_of(j * tk, tk)
        k = k_ref[pl.ds(k0, tk), :]
        v = v_ref[pl.ds(k0, tk), :]
        s = lax.dot_general(q, k, (((1,), (1,)), ((), ())), preferred_element_type=F32)
        s = s + (fq - ft_ref[:, pl.ds(k0, tk)])
        if masked:
            rows = lax.broadcasted_iota(jnp.int32, (tq, tk), 0)
            cols = lax.broadcasted_iota(jnp.int32, (tq, tk), 1)
            s = jnp.where(rows >= cols, s, MASK_VALUE)
        m_prev = m_ref[...]
        m_new = jnp.maximum(m_prev, jnp.max(s, axis=1, keepdims=True))
        alpha = jnp.exp(m_prev - m_new)
        p = jnp.exp(s - jnp.concatenate([m_new] * reps, axis=1))
        l_ref[...] = alpha * l_ref[...] + jnp.sum(p, axis=1, keepdims=True)
        acc_ref[...] = alpha * acc_ref[...] + jnp.dot(p.astype(BF16), v,
                                                      preferred_element_type=F32)
        m_ref[...] = m_new

    def loop_body(j, carry):
        block(j, False)
        return carry

    lax.fori_loop(0, i, loop_body, 0)
    block(i, True)

    att = acc_ref[...] / l_ref[...]
    o_ref[...] = (att * z_ref[...].astype(F32)).astype(o_ref.dtype)


def _attention(qk, vz, ft, *, heads, tq=512):
    s = qk.shape[0]
    tk = tq
    return pl.pallas_call(
        functools.partial(_attn_kernel, tq=tq, tk=tk),
        grid=(heads, s // tq),
        in_specs=[pl.BlockSpec((tq, HEAD_DIM), lambda h, i: (i, h)),
                  pl.BlockSpec((s, HEAD_DIM), lambda h, i: (0, heads + h)),
                  pl.BlockSpec((s, HEAD_DIM), lambda h, i: (0, h)),
                  pl.BlockSpec((None, 1, s), lambda h, i: (h, 0, 0)),
                  pl.BlockSpec((tq, HEAD_DIM), lambda h, i: (i, heads + h))],
        out_specs=pl.BlockSpec((tq, HEAD_DIM), lambda h, i: (i, h)),
        out_shape=jax.ShapeDtypeStruct((s, heads * HEAD_DIM), BF16),
        scratch_shapes=[pltpu.VMEM((tq, LANES), F32),
                        pltpu.VMEM((tq, LANES), F32),
                        pltpu.VMEM((tq, HEAD_DIM), F32)],
        compiler_params=_params(("parallel", "arbitrary"), 40),
        name="attn",
    )(qk, qk, vz, ft, vz)


def _rnn_kernel(x_ref, z_ref, cw_ref, cb_ref, wax_ref, ba_ref, bx_ref, lam_ref, o_ref,
                xprev_ref, hc_ref, *, tt):
    t = pl.program_id(1)

    @pl.when(t == 0)
    def _():
        xprev_ref[...] = jnp.zeros(xprev_ref.shape, F32)
        hc_ref[...] = jnp.zeros(hc_ref.shape, F32)

    x = x_ref[...]
    xp = xprev_ref[...]
    row8 = lax.broadcasted_iota(jnp.int32, (8, LANES), 0)
    xc = cb_ref[...] + x * cw_ref[CONV_W - 1:CONV_W, :]
    for j in range(1, CONV_W):
        r = pltpu.roll(x, j, 0)
        head = jnp.where(row8 < j, pltpu.roll(xp, j, 0), r[:8])
        xs = jnp.concatenate([head, r[8:]], axis=0)
        xc = xc + xs * cw_ref[CONV_W - 1 - j:CONV_W - j, :]
    xprev_ref[...] = x[tt - 8:, :]

    g = jnp.dot(xc.astype(BF16), wax_ref[...], preferred_element_type=F32)
    r_gate = _sigmoid(g[:, :LANES] + ba_ref[...])
    i_gate = _sigmoid(g[:, LANES:] + bx_ref[...])
    log_a = (-RG_C) * r_gate * _softplus(-lam_ref[...])
    a = jnp.exp(log_a)
    mult = jnp.sqrt(-jnp.tanh(log_a) * (a * a + 1.0))
    b = mult * (i_gate * xc)

    row = lax.broadcasted_iota(jnp.int32, (tt, LANES), 0)
    d = 1
    while d < tt:
        keep = row >= d
        b = jnp.where(keep, a * pltpu.roll(b, d, 0) + b, b)
        a = jnp.where(keep, a * pltpu.roll(a, d, 0), a)
        d *= 2
    h = b + a * hc_ref[...]
    hc_ref[...] = h[tt - 1:tt, :]
    o_ref[...] = (h * z_ref[...].astype(F32)).astype(o_ref.dtype)


def _rnn(x_r, z_r, conv_w, conv_b, w_ax, b_a, b_x, lam, *, tt=2048):
    s, w = x_r.shape
    nblk = w // LANES
    tile = pl.BlockSpec((tt, LANES), lambda n, t: (t, n))
    vec = pl.BlockSpec((1, LANES), lambda n, t: (0, n))
    return pl.pallas_call(
        functools.partial(_rnn_kernel, tt=tt),
        grid=(nblk, s // tt),
        in_specs=[tile, tile,
                  pl.BlockSpec((CONV_W, LANES), lambda n, t: (0, n)), vec,
                  pl.BlockSpec((None, LANES, 2 * LANES), lambda n, t: (n, 0, 0)),
                  vec, vec, vec],
        out_specs=tile,
        out_shape=jax.ShapeDtypeStruct((s, w), BF16),
        scratch_shapes=[pltpu.VMEM((8, LANES), F32), pltpu.VMEM((1, LANES), F32)],
        compiler_params=_params(("parallel", "arbitrary"), 40),
        name="rnn",
    )(x_r, z_r, conv_w, conv_b.reshape(1, w), w_ax, b_a.reshape(1, w), b_x.reshape(1, w),
      lam.reshape(1, w))


def _merge_kernel(h_ref, ya_ref, yr_ref, wga_ref, wgr_ref, wba_ref, wbr_ref, bga_ref, bgr_ref,
                  o_ref):
    h = h_ref[...]
    g_a = _sigmoid(jnp.dot(h, wga_ref[...], preferred_element_type=F32) + bga_ref[...])
    y_a = jnp.dot(ya_ref[...], wba_ref[...], preferred_element_type=F32)
    acc = g_a * y_a
    g_r = _sigmoid(jnp.dot(h, wgr_ref[...], preferred_element_type=F32) + bgr_ref[...])
    y_r = jnp.dot(yr_ref[...], wbr_ref[...], preferred_element_type=F32)
    o_ref[...] = (acc + g_r * y_r).astype(o_ref.dtype)


def _merge(h, ya, yr, w_gate, w_ba, w_br, b_gate, *, tm=512, tn=256):
    s, d = h.shape
    wd = ya.shape[1]
    nj = d // tn
    b_gate = b_gate.reshape(1, 2 * d)
    return pl.pallas_call(
        _merge_kernel,
        grid=(s // tm, d // tn),
        in_specs=[pl.BlockSpec((tm, d), lambda i, j: (i, 0)),
                  pl.BlockSpec((tm, wd), lambda i, j: (i, 0)),
                  pl.BlockSpec((tm, wd), lambda i, j: (i, 0)),
                  pl.BlockSpec((d, tn), lambda i, j: (0, j)),
                  pl.BlockSpec((d, tn), lambda i, j: (0, nj + j)),
                  pl.BlockSpec((wd, tn), lambda i, j: (0, j)),
                  pl.BlockSpec((wd, tn), lambda i, j: (0, j)),
                  pl.BlockSpec((1, tn), lambda i, j: (0, j)),
                  pl.BlockSpec((1, tn), lambda i, j: (0, nj + j))],
        out_specs=pl.BlockSpec((tm, tn), lambda i, j: (i, j)),
        out_shape=jax.ShapeDtypeStruct((s, d), BF16),
        compiler_params=_params(("parallel", "arbitrary"), 52),
        name="merge",
    )(h, ya, yr, w_gate, w_gate, w_ba, w_br, b_gate, b_gate)


def kernel(x, c, w_ada, b_ada, norm_w, w_in, b_f, q_norm_w, k_norm_w, conv_w, conv_b, w_rg_a,
           b_rg_a, w_rg_x, b_rg_x, lru_lambda, w_br_attn, w_br_rnn, w_gate, b_gate, w_out):
    bsz, s, d = x.shape
    assert bsz == 1
    aw = w_br_attn.shape[0]
    rw = w_br_rnn.shape[0]
    heads = aw // HEAD_DIM
    x2 = x.reshape(s, d)

    mod = _ada(c, w_ada, b_ada)
    shift, scale, gate_res = mod[:, :d], mod[:, d:2 * d], mod[:, 2 * d:]
    h = _prenorm(x2, norm_w, scale, shift)

    o_f = 4 * aw
    o_xr = o_f + heads
    w_qk = w_in[:, :2 * aw].astype(BF16)
    w_vz = w_in[:, 2 * aw:4 * aw].astype(BF16)
    w_f = jnp.pad(w_in[:, o_f:o_xr], ((0, 0), (0, LANES - heads))).astype(BF16)
    w_xr = w_in[:, o_xr:o_xr + rw].astype(BF16)
    w_zr = w_in[:, o_xr + rw:].astype(BF16)

    qk_gain = jnp.concatenate([jnp.tile(q_norm_w * (1.0 / math.sqrt(HEAD_DIM)), heads),
                               jnp.tile(k_norm_w, heads)])
    qk = _proj(h, w_qk, out_dtype=BF16, epilogue="headnorm", tm=1024, tn=512,
               row_vecs=(qk_gain,), name="proj_qk")
    vz = _proj(h, w_vz, out_dtype=BF16, epilogue="plain_then_silu", tm=1024, tn=512,
               silu_from=aw // 512, name="proj_vz")
    logf = _proj(h, w_f, out_dtype=F32, epilogue="logsigmoid", tm=1024, tn=LANES,
                 row_vecs=(jnp.pad(b_f, (0, LANES - heads)),), name="proj_f")
    x_r = _proj(h, w_xr, out_dtype=F32, epilogue="plain", tm=1024, tn=512, name="proj_xr")
    z_r = _proj(h, w_zr, out_dtype=BF16, epilogue="silu", tm=1024, tn=512, name="proj_zr")

    ft = _cumsum_t(logf)[:heads].reshape(heads, 1, s)
    ya = _attention(qk, vz, ft, heads=heads)

    w_ax = jnp.concatenate([w_rg_a, w_rg_x], axis=2).astype(BF16)
    yr = _rnn(x_r, z_r, conv_w, conv_b, w_ax, b_rg_a, b_rg_x, lru_lambda)

    merged = _merge(h, ya, yr, w_gate.astype(BF16), w_br_attn.astype(BF16),
                    w_br_rnn.astype(BF16), b_gate)
    y = _proj(merged, w_out.astype(BF16), out_dtype=x.dtype, epilogue="residual", tm=1024,
              tn=512, tiles=(x2,), row_vecs=(gate_res.reshape(-1),), name="out")
    return y.reshape(bsz, s, d)
```

```python
import functools
import math

import jax
import jax.numpy as jnp
from jax import lax
from jax.experimental import pallas as pl
from jax.experimental.pallas import tpu as pltpu

F32 = jnp.float32
BF16 = jnp.bfloat16

LANES = 128
SUBLANES = 8
HEAD_DIM = 128
CONV_W = 4
RG_C = 8.0
EPS = 1e-6
MASK_VALUE = -1e30
LOG2E = 1.4426950408889634
MIB = 1024 * 1024


def _params(semantics, vmem_mib):
    return pltpu.CompilerParams(dimension_semantics=semantics,
                                vmem_limit_bytes=vmem_mib * MIB)


def _sigmoid(x):
    return 1.0 / (1.0 + jnp.exp(-x))


def _silu(x):
    return x * _sigmoid(x)


def _softplus(x):
    return jnp.maximum(x, 0.0) + jnp.log1p(jnp.exp(-jnp.abs(x)))


def _ada_kernel(c_ref, w_ref, b_ref, o_ref, *, rows):
    d = w_ref.shape[0]

    def body(r, acc):
        r0 = pl.multiple_of(r * rows, rows)
        c = c_ref[pl.ds(r0, rows), :]
        return acc + jnp.sum(w_ref[pl.ds(r0, rows), :] * _silu(c), axis=0, keepdims=True)

    acc = lax.fori_loop(0, d // rows, body, jnp.zeros(o_ref.shape, F32))
    o_ref[...] = acc + b_ref[...]


def _ada(c, w_ada, b_ada, *, tn=512, rows=512):
    d, n = w_ada.shape
    return pl.pallas_call(
        functools.partial(_ada_kernel, rows=rows),
        grid=(n // tn,),
        in_specs=[pl.BlockSpec((d, 1), lambda j: (0, 0)),
                  pl.BlockSpec((d, tn), lambda j: (0, j)),
                  pl.BlockSpec((1, tn), lambda j: (0, j))],
        out_specs=pl.BlockSpec((1, tn), lambda j: (0, j)),
        out_shape=jax.ShapeDtypeStruct((1, n), F32),
        compiler_params=_params(("parallel",), 40),
        name="ada",
    )(c.reshape(d, 1), w_ada, b_ada.reshape(1, n))


def _prenorm_kernel(x_ref, nw_ref, scale_ref, shift_ref, o_ref):
    x = x_ref[...]
    ms = jnp.mean(x * x, axis=-1, keepdims=True)
    h = x * lax.rsqrt(ms + EPS) * nw_ref[...]
    o_ref[...] = (h * (1.0 + scale_ref[...]) + shift_ref[...]).astype(o_ref.dtype)


def _prenorm(x, norm_w, scale, shift, *, tm=256):
    s, d = x.shape
    vec = pl.BlockSpec((1, d), lambda i: (0, 0))
    return pl.pallas_call(
        _prenorm_kernel,
        grid=(s // tm,),
        in_specs=[pl.BlockSpec((tm, d), lambda i: (i, 0)), vec, vec, vec],
        out_specs=pl.BlockSpec((tm, d), lambda i: (i, 0)),
        out_shape=jax.ShapeDtypeStruct((s, d), BF16),
        compiler_params=_params(("parallel",), 40),
        name="prenorm",
    )(x, norm_w.reshape(1, d), scale, shift)


def _cast_weight_tile(w_ref, wn_ref, wb_ref, *, lane_shift, rows=512):
    k, tn = wb_ref.shape

    def body(r, carry):
        r0 = pl.multiple_of(r * rows, rows)
        w = w_ref[pl.ds(r0, rows), :]
        if lane_shift:
            w = jnp.concatenate([w, wn_ref[pl.ds(r0, rows), :]], axis=1)
            w = pltpu.roll(w, tn + LANES - lane_shift, 1)[:, :tn]
        wb_ref[pl.ds(r0, rows), :] = w.astype(BF16)
        return carry

    lax.fori_loop(0, k // rows, body, 0)


def _epilogue(y, rest, o_ref, epilogue):
    if epilogue == "plain":
        o_ref[...] = y.astype(o_ref.dtype)
    elif epilogue == "silu":
        o_ref[...] = _silu(y).astype(o_ref.dtype)
    elif epilogue == "sigmoid_bias":
        o_ref[...] = _sigmoid(y + rest[0][...]).astype(o_ref.dtype)
    elif epilogue == "headnorm":
        gain_ref = rest[0]
        for hh in range(y.shape[1] // HEAD_DIM):
            sl = slice(hh * HEAD_DIM, (hh + 1) * HEAD_DIM)
            yh = y[:, sl]
            ms = jnp.mean(yh * yh, axis=-1, keepdims=True)
            o_ref[:, sl] = (yh * lax.rsqrt(ms + EPS) * gain_ref[:, sl]).astype(o_ref.dtype)
    elif epilogue == "logsigmoid":
        z = y + rest[0][...]
        o_ref[...] = (jnp.minimum(z, 0.0) - jnp.log1p(jnp.exp(-jnp.abs(z)))).astype(o_ref.dtype)
    elif epilogue == "residual":
        x_ref, g_ref = rest[0], rest[1]
        o_ref[...] = (x_ref[...] + g_ref[...] * y).astype(o_ref.dtype)
    else:
        raise ValueError(epilogue)


def _proj_kernel(a_ref, w_ref, *rest, epilogue, lane_shift):
    wb_ref = rest[-1]
    o_ref = rest[-2]
    rest = rest[:-2]
    wn_ref = None
    if lane_shift:
        wn_ref, rest = rest[0], rest[1:]

    @pl.when(pl.program_id(1) == 0)
    def _():
        _cast_weight_tile(w_ref, wn_ref, wb_ref, lane_shift=lane_shift)

    y = jnp.dot(a_ref[...], wb_ref[...], preferred_element_type=F32)
    _epilogue(y, rest, o_ref, epilogue)


def _proj(a, w, *, n, col0, out_dtype, epilogue, tm, tn, row_vecs=(), tiles=(), name):
    m, k = a.shape
    lane_shift = col0 % LANES
    base = col0 - lane_shift
    assert base % tn == 0 and n % tn == 0 and m % tm == 0
    jb = base // tn
    in_specs = [pl.BlockSpec((tm, k), lambda j, i: (i, 0)),
                pl.BlockSpec((k, tn), lambda j, i: (0, jb + j))]
    args = [a, w]
    if lane_shift:
        per = tn // LANES
        in_specs.append(pl.BlockSpec((k, LANES), lambda j, i: (0, (jb + j + 1) * per)))
        args.append(w)
    for t in tiles:
        in_specs.append(pl.BlockSpec((tm, tn), lambda j, i: (i, j)))
        args.append(t)
    for v in row_vecs:
        in_specs.append(pl.BlockSpec((1, tn), lambda j, i: (0, j)))
        args.append(v.reshape(1, n))
    return pl.pallas_call(
        functools.partial(_proj_kernel, epilogue=epilogue, lane_shift=lane_shift),
        grid=(n // tn, m // tm),
        in_specs=in_specs,
        out_specs=pl.BlockSpec((tm, tn), lambda j, i: (i, j)),
        out_shape=jax.ShapeDtypeStruct((m, n), out_dtype),
        scratch_shapes=[pltpu.VMEM((k, tn), BF16)],
        compiler_params=_params(("parallel", "arbitrary"), 56),
        name=name,
    )(*args)


def _cumsum_kernel(x_ref, ft_ref, *, chunk):
    s = x_ref.shape[0]
    row = lax.broadcasted_iota(jnp.int32, (chunk, LANES), 0)

    def body(c, carry):
        s0 = pl.multiple_of(c * chunk, chunk)
        x = x_ref[pl.ds(s0, chunk), :]
        d = 1
        while d < chunk:
            x = x + jnp.where(row >= d, pltpu.roll(x, d, 0), 0.0)
            d *= 2
        x = x + carry
        ft_ref[:, pl.ds(s0, chunk)] = x.T
        return x[chunk - 1:chunk, :]

    lax.fori_loop(0, s // chunk, body, jnp.zeros((1, LANES), F32))


def _cumsum_t(logf, *, chunk=512):
    s = logf.shape[0]
    return pl.pallas_call(
        functools.partial(_cumsum_kernel, chunk=chunk),
        out_shape=jax.ShapeDtypeStruct((LANES, s), F32),
        compiler_params=_params(None, 40),
        name="cumsum",
    )(logf)


N_SPLIT = 3


def _decay_features(ft_ref, qf_ref, kf_ref, *, chunk=512):
    s = ft_ref.shape[1]
    row = lax.broadcasted_iota(jnp.int32, (LANES, chunk), 0)

    def body(c, carry):
        c0 = pl.multiple_of(c * chunk, chunk)
        rem = ft_ref[:, pl.ds(c0, chunk)] * LOG2E
        qf = jnp.where((row >= N_SPLIT) & (row < 2 * N_SPLIT), 1.0, 0.0)
        kf = jnp.where(row < N_SPLIT, 1.0, 0.0)
        for p in range(N_SPLIT):
            piece = rem.astype(BF16).astype(F32)
            rem = rem - piece
            qf = jnp.where(row == p, piece, qf)
            kf = jnp.where(row == N_SPLIT + p, -piece, kf)
        qf_ref[pl.ds(c0, chunk), :] = qf.T.astype(BF16)
        kf_ref[pl.ds(c0, chunk), :] = kf.T.astype(BF16)
        return carry

    lax.fori_loop(0, s // chunk, body, 0)


def _attn_kernel(q_ref, k_ref, v_ref, ft_ref, z_ref, o_ref, qf_ref, kf_ref, qa_ref, sa_ref, sb_ref,
                 m_ref, l_ref, acc_ref, *, tq, tk):
    i = pl.program_id(1)
    reps = tk // LANES

    @pl.when(i == 0)
    def _():
        _decay_features(ft_ref, qf_ref, kf_ref)

    q0 = pl.multiple_of(i * tq, tq)
    qa_ref[:, :HEAD_DIM] = q_ref[...]
    qa_ref[:, HEAD_DIM:] = qf_ref[pl.ds(q0, tq), :]

    m_ref[...] = jnp.full(m_ref.shape, MASK_VALUE, F32)
    l_ref[...] = jnp.zeros(l_ref.shape, F32)
    acc_ref[...] = jnp.zeros(acc_ref.shape, F32)

    def scores(j, dst_ref, r0=0):
        k0 = pl.multiple_of(j * tk, tk)
        ka = jnp.concatenate([k_ref[pl.ds(k0, tk), :], kf_ref[pl.ds(k0, tk), :]], axis=1)
        dst_ref[r0:, :] = lax.dot_general(qa_ref[r0:, :], ka, (((1,), (1,)), ((), ())),
                                          preferred_element_type=F32)

    def consume(src_ref, j, r0=0, causal=False):
        k0 = pl.multiple_of(j * tk, tk)
        nr = tq - r0
        s2 = src_ref[r0:, :]
        if causal:
            rows = lax.broadcasted_iota(jnp.int32, (nr, tk), 0)
            cols = lax.broadcasted_iota(jnp.int32, (nr, tk), 1)
            s2 = jnp.where(rows >= cols, s2, MASK_VALUE)
        m_prev = m_ref[r0:, :]
        m_new = jnp.maximum(m_prev, jnp.max(s2, axis=1, keepdims=True))
        alpha = jnp.exp2(m_prev - m_new)
        p = jnp.exp2(s2 - jnp.concatenate([m_new] * reps, axis=1))
        psum = p[:, :LANES]
        for cc in range(1, reps):
            psum = psum + p[:, cc * LANES:(cc + 1) * LANES]
        l_ref[r0:, :] = alpha * l_ref[r0:, :] + psum
        acc_ref[r0:, :] = alpha * acc_ref[r0:, :] + jnp.dot(
            p.astype(BF16), v_ref[pl.ds(k0, tk), :], preferred_element_type=F32)
        m_ref[r0:, :] = m_new

    scores(0, sa_ref)

    def pair(jj, carry):
        j = 2 * jj
        scores(j + 1, sb_ref)
        consume(sa_ref, j)
        scores(j + 2, sa_ref)
        consume(sb_ref, j + 1)
        return carry

    lax.fori_loop(0, i, pair, 0)
    scores(2 * i + 1, sb_ref, r0=tk)
    consume(sa_ref, 2 * i, causal=True)
    consume(sb_ref, 2 * i + 1, r0=tk, causal=True)

    l = jnp.sum(l_ref[...], axis=1, keepdims=True)
    o_ref[...] = (acc_ref[...] / l * z_ref[...].astype(F32)).astype(o_ref.dtype)


def _attention(qk, v, z, ft, *, heads, tq=1024):
    s = qk.shape[0]
    tk = tq // 2
    return pl.pallas_call(
        functools.partial(_attn_kernel, tq=tq, tk=tk),
        grid=(heads, s // tq),
        in_specs=[pl.BlockSpec((tq, HEAD_DIM), lambda h, i: (i, h)),
                  pl.BlockSpec((s, HEAD_DIM), lambda h, i: (0, heads + h)),
                  pl.BlockSpec((s, HEAD_DIM), lambda h, i: (0, h)),
                  pl.BlockSpec((None, 1, s), lambda h, i: (h, 0, 0)),
                  pl.BlockSpec((tq, HEAD_DIM), lambda h, i: (i, h))],
        out_specs=pl.BlockSpec((tq, HEAD_DIM), lambda h, i: (i, h)),
        out_shape=jax.ShapeDtypeStruct((s, heads * HEAD_DIM), BF16),
        scratch_shapes=[pltpu.VMEM((s, LANES), BF16),
                        pltpu.VMEM((s, LANES), BF16),
                        pltpu.VMEM((tq, HEAD_DIM + LANES), BF16),
                        pltpu.VMEM((tq, tk), F32),
                        pltpu.VMEM((tq, tk), F32),
                        pltpu.VMEM((tq, LANES), F32),
                        pltpu.VMEM((tq, LANES), F32),
                        pltpu.VMEM((tq, HEAD_DIM), F32)],
        compiler_params=_params(("parallel", "arbitrary"), 40),
        name="attn",
    )(qk, qk, v, ft, z)


def _rnn_kernel(x_ref, z_ref, cw_ref, cb_ref, wax_ref, ba_ref, bx_ref, lam_ref, o_ref,
                xprev_ref, hc_ref, a_ref, b_ref, *, tt, wt):
    t = pl.program_id(1)
    groups = tt // SUBLANES

    @pl.when(t == 0)
    def _():
        xprev_ref[...] = jnp.zeros(xprev_ref.shape, F32)
        hc_ref[...] = jnp.zeros(hc_ref.shape, F32)

    x = x_ref[...]
    xp = xprev_ref[...]
    row8 = lax.broadcasted_iota(jnp.int32, (SUBLANES, wt), 0)
    xc = cb_ref[...] + x * cw_ref[CONV_W - 1:CONV_W, :]
    for j in range(1, CONV_W):
        r = pltpu.roll(x, j, 0)
        head = jnp.where(row8 < j, pltpu.roll(xp, j, 0), r[:SUBLANES])
        xs = jnp.concatenate([head, r[SUBLANES:]], axis=0)
        xc = xc + xs * cw_ref[CONV_W - 1 - j:CONV_W - j, :]
    xprev_ref[...] = x[tt - SUBLANES:, :]

    xcb = xc.astype(BF16)
    ga, gx = [], []
    for n in range(wt // LANES):
        g = jnp.dot(xcb[:, n * LANES:(n + 1) * LANES], wax_ref[n], preferred_element_type=F32)
        ga.append(g[:, :LANES])
        gx.append(g[:, LANES:])
    r_gate = _sigmoid(jnp.concatenate(ga, axis=1) + ba_ref[...])
    i_gate = _sigmoid(jnp.concatenate(gx, axis=1) + bx_ref[...])
    log_a = (-RG_C) * r_gate * _softplus(-lam_ref[...])
    a = jnp.exp(log_a)
    mult = jnp.sqrt(-jnp.tanh(log_a) * (a * a + 1.0))
    b = mult * (i_gate * xc)

    a3 = a.reshape(groups, SUBLANES, wt)
    b3 = b.reshape(groups, SUBLANES, wt)
    sub = lax.broadcasted_iota(jnp.int32, (groups, SUBLANES, wt), 1)
    d = 1
    while d < SUBLANES:
        keep = sub >= d
        b3 = a3 * jnp.where(keep, pltpu.roll(b3, d, 1), 0.0) + b3
        a3 = a3 * jnp.where(keep, pltpu.roll(a3, d, 1), 1.0)
        d *= 2
    a_ref[...] = a3.reshape(tt, wt)
    b_ref[...] = b3.reshape(tt, wt)

    def body(g, hprev):
        g0 = pl.multiple_of(g * SUBLANES, SUBLANES)
        hg = b_ref[pl.ds(g0, SUBLANES), :] + a_ref[pl.ds(g0, SUBLANES), :] * hprev
        b_ref[pl.ds(g0, SUBLANES), :] = hg
        return jnp.broadcast_to(hg[SUBLANES - 1:SUBLANES, :], (SUBLANES, wt))

    hlast = lax.fori_loop(0, groups, body, hc_ref[...], unroll=8)
    hc_ref[...] = hlast
    o_ref[...] = (b_ref[...] * z_ref[...].astype(F32)).astype(o_ref.dtype)


def _rnn(x_r, z_r, conv_w, conv_b, w_ax, b_a, b_x, lam, *, tt=1024, wt=512):
    s, w = x_r.shape
    nb = wt // LANES
    tile = pl.BlockSpec((tt, wt), lambda n, t: (t, n))
    vec = pl.BlockSpec((1, wt), lambda n, t: (0, n))
    return pl.pallas_call(
        functools.partial(_rnn_kernel, tt=tt, wt=wt),
        grid=(w // wt, s // tt),
        in_specs=[tile, tile,
                  pl.BlockSpec((CONV_W, wt), lambda n, t: (0, n)), vec,
                  pl.BlockSpec((nb, LANES, 2 * LANES), lambda n, t: (n, 0, 0)),
                  vec, vec, vec],
        out_specs=tile,
        out_shape=jax.ShapeDtypeStruct((s, w), BF16),
        scratch_shapes=[pltpu.VMEM((SUBLANES, wt), F32),
                        pltpu.VMEM((SUBLANES, wt), F32),
                        pltpu.VMEM((tt, wt), F32),
                        pltpu.VMEM((tt, wt), F32)],
        compiler_params=_params(("parallel", "arbitrary"), 40),
        name="rnn",
    )(x_r, z_r, conv_w, conv_b.reshape(1, w), w_ax, b_a.reshape(1, w), b_x.reshape(1, w),
      lam.reshape(1, w))


def _merge_kernel(ya_ref, yr_ref, ga_ref, gr_ref, wba_ref, wbr_ref, o_ref, wa_ref, wr_ref):
    @pl.when(pl.program_id(1) == 0)
    def _():
        _cast_weight_tile(wba_ref, None, wa_ref, lane_shift=0)
        _cast_weight_tile(wbr_ref, None, wr_ref, lane_shift=0)

    y_a = jnp.dot(ya_ref[...], wa_ref[...], preferred_element_type=F32)
    acc = ga_ref[...].astype(F32) * y_a
    y_r = jnp.dot(yr_ref[...], wr_ref[...], preferred_element_type=F32)
    o_ref[...] = (acc + gr_ref[...].astype(F32) * y_r).astype(o_ref.dtype)


def _merge(ya, yr, g, w_ba, w_br, *, tm=1024, tn=512):
    s, wd = ya.shape
    d = w_ba.shape[1]
    nj = d // tn
    return pl.pallas_call(
        _merge_kernel,
        grid=(d // tn, s // tm),
        in_specs=[pl.BlockSpec((tm, wd), lambda j, i: (i, 0)),
                  pl.BlockSpec((tm, wd), lambda j, i: (i, 0)),
                  pl.BlockSpec((tm, tn), lambda j, i: (i, j)),
                  pl.BlockSpec((tm, tn), lambda j, i: (i, nj + j)),
                  pl.BlockSpec((wd, tn), lambda j, i: (0, j)),
                  pl.BlockSpec((wd, tn), lambda j, i: (0, j))],
        out_specs=pl.BlockSpec((tm, tn), lambda j, i: (i, j)),
        out_shape=jax.ShapeDtypeStruct((s, d), BF16),
        scratch_shapes=[pltpu.VMEM((wd, tn), BF16), pltpu.VMEM((wd, tn), BF16)],
        compiler_params=_params(("parallel", "arbitrary"), 56),
        name="merge",
    )(ya, yr, g, g, w_ba, w_br)


def kernel(x, c, w_ada, b_ada, norm_w, w_in, b_f, q_norm_w, k_norm_w, conv_w, conv_b, w_rg_a,
           b_rg_a, w_rg_x, b_rg_x, lru_lambda, w_br_attn, w_br_rnn, w_gate, b_gate, w_out):
    bsz, s, d = x.shape
    assert bsz == 1
    aw = w_br_attn.shape[0]
    rw = w_br_rnn.shape[0]
    heads = aw // HEAD_DIM
    x2 = x.reshape(s, d)

    mod = _ada(c, w_ada, b_ada)
    shift, scale, gate_res = mod[:, :d], mod[:, d:2 * d], mod[:, 2 * d:]
    h = _prenorm(x2, norm_w, scale, shift)

    o_f = 4 * aw
    o_xr = o_f + heads
    qk_gain = jnp.concatenate([jnp.tile(q_norm_w * (LOG2E / math.sqrt(HEAD_DIM)), heads),
                               jnp.tile(k_norm_w, heads)])
    proj = functools.partial(_proj, h, w_in, tm=1024, tn=512)
    qk = proj(n=2 * aw, col0=0, out_dtype=BF16, epilogue="headnorm", row_vecs=(qk_gain,),
              name="proj_qk")
    v = proj(n=aw, col0=2 * aw, out_dtype=BF16, epilogue="plain", name="proj_v")
    z_a = proj(n=aw, col0=3 * aw, out_dtype=BF16, epilogue="silu", name="proj_za")
    logf = _proj(h, w_in, n=LANES, col0=o_f, out_dtype=F32, epilogue="logsigmoid", tm=1024,
                 tn=LANES, row_vecs=(jnp.pad(b_f, (0, LANES - heads)),), name="proj_f")
    x_r = proj(n=rw, col0=o_xr, out_dtype=F32, epilogue="plain", name="proj_xr")
    z_r = proj(n=rw, col0=o_xr + rw, out_dtype=BF16, epilogue="silu", name="proj_zr")

    ft = _cumsum_t(logf)[:heads].reshape(heads, 1, s)
    ya = _attention(qk, v, z_a, ft, heads=heads)

    w_ax = jnp.concatenate([w_rg_a, w_rg_x], axis=2).astype(BF16)
    yr = _rnn(x_r, z_r, conv_w, conv_b, w_ax, b_rg_a, b_rg_x, lru_lambda)

    g = _proj(h, w_gate, n=2 * d, col0=0, out_dtype=BF16, epilogue="sigmoid_bias", tm=1024, tn=512,
              row_vecs=(b_gate,), name="gate")
    merged = _merge(ya, yr, g, w_br_attn, w_br_rnn)
    y = _proj(merged, w_out, n=d, col0=0, out_dtype=x.dtype, epilogue="residual", tm=1024, tn=512,
              tiles=(x2,), row_vecs=(gate_res.reshape(-1),), name="out")
    return y.reshape(bsz, s, d)
```

```python
import functools
import math

import jax
import jax.numpy as jnp
from jax import lax
from jax.experimental import pallas as pl
from jax.experimental.pallas import tpu as pltpu

F32 = jnp.float32
BF16 = jnp.bfloat16

LANES = 128
SUBLANES = 8
HEAD_DIM = 128
CONV_W = 4
RG_C = 8.0
EPS = 1e-6
MASK_VALUE = -1e30
LOG2E = 1.4426950408889634
MIB = 1024 * 1024


def _params(semantics, vmem_mib):
    return pltpu.CompilerParams(dimension_semantics=semantics,
                                vmem_limit_bytes=vmem_mib * MIB)


def _sigmoid(x):
    return 1.0 / (1.0 + jnp.exp(-x))


def _silu(x):
    return x * _sigmoid(x)


def _softplus(x):
    return jnp.maximum(x, 0.0) + jnp.log1p(jnp.exp(-jnp.abs(x)))


def _ada_kernel(c_ref, w_ref, b_ref, o_ref, *, rows):
    d = w_ref.shape[0]

    def body(r, acc):
        r0 = pl.multiple_of(r * rows, rows)
        c = c_ref[pl.ds(r0, rows), :]
        return acc + jnp.sum(w_ref[pl.ds(r0, rows), :] * _silu(c), axis=0, keepdims=True)

    acc = lax.fori_loop(0, d // rows, body, jnp.zeros(o_ref.shape, F32))
    o_ref[...] = acc + b_ref[...]


def _ada(c, w_ada, b_ada, *, tn=512, rows=512):
    d, n = w_ada.shape
    return pl.pallas_call(
        functools.partial(_ada_kernel, rows=rows),
        grid=(n // tn,),
        in_specs=[pl.BlockSpec((d, 1), lambda j: (0, 0)),
                  pl.BlockSpec((d, tn), lambda j: (0, j)),
                  pl.BlockSpec((1, tn), lambda j: (0, j))],
        out_specs=pl.BlockSpec((1, tn), lambda j: (0, j)),
        out_shape=jax.ShapeDtypeStruct((1, n), F32),
        compiler_params=_params(("parallel",), 40),
        name="ada",
    )(c.reshape(d, 1), w_ada, b_ada.reshape(1, n))


def _prenorm_kernel(x_ref, nw_ref, scale_ref, shift_ref, o_ref):
    x = x_ref[...]
    ms = jnp.mean(x * x, axis=-1, keepdims=True)
    h = x * lax.rsqrt(ms + EPS) * nw_ref[...]
    o_ref[...] = (h * (1.0 + scale_ref[...]) + shift_ref[...]).astype(o_ref.dtype)


def _prenorm(x, norm_w, scale, shift, *, tm=256):
    s, d = x.shape
    vec = pl.BlockSpec((1, d), lambda i: (0, 0))
    return pl.pallas_call(
        _prenorm_kernel,
        grid=(s // tm,),
        in_specs=[pl.BlockSpec((tm, d), lambda i: (i, 0)), vec, vec, vec],
        out_specs=pl.BlockSpec((tm, d), lambda i: (i, 0)),
        out_shape=jax.ShapeDtypeStruct((s, d), BF16),
        compiler_params=_params(("parallel",), 40),
        name="prenorm",
    )(x, norm_w.reshape(1, d), scale, shift)


def _cast_weight_tile(w_ref, wb_ref, *, rows=128):
    n_rows = w_ref.shape[0]

    def body(r, carry):
        r0 = pl.multiple_of(r * rows, rows)
        wb_ref[pl.ds(r0, rows), :] = w_ref[pl.ds(r0, rows), :].astype(BF16)
        return carry

    lax.fori_loop(0, n_rows // rows, body, 0)


def _epilogue(y, rest, o_ref, epilogue):
    if epilogue == "plain":
        o_ref[...] = y.astype(o_ref.dtype)
    elif epilogue == "silu":
        o_ref[...] = _silu(y).astype(o_ref.dtype)
    elif epilogue == "sigmoid_bias":
        o_ref[...] = _sigmoid(y + rest[0][...]).astype(o_ref.dtype)
    elif epilogue == "headnorm":
        gain_ref = rest[0]
        for hh in range(y.shape[1] // HEAD_DIM):
            sl = slice(hh * HEAD_DIM, (hh + 1) * HEAD_DIM)
            yh = y[:, sl]
            ms = jnp.mean(yh * yh, axis=-1, keepdims=True)
            o_ref[:, sl] = (yh * lax.rsqrt(ms + EPS) * gain_ref[:, sl]).astype(o_ref.dtype)
    elif epilogue == "logsigmoid":
        z = y + rest[0][...]
        o_ref[...] = (jnp.minimum(z, 0.0) - jnp.log1p(jnp.exp(-jnp.abs(z)))).astype(o_ref.dtype)
    elif epilogue == "residual":
        x_ref, g_ref = rest[0], rest[1]
        o_ref[...] = (x_ref[...] + g_ref[...] * y).astype(o_ref.dtype)
    else:
        raise ValueError(epilogue)


def _proj_kernel(a_ref, w_ref, *rest, epilogue, w_transposed):
    wb_ref = rest[-1]
    o_ref = rest[-2]

    @pl.when(pl.program_id(1) == 0)
    def _():
        _cast_weight_tile(w_ref, wb_ref)

    if w_transposed:
        y = lax.dot_general(a_ref[...], wb_ref[...], (((1,), (1,)), ((), ())),
                            preferred_element_type=F32)
    else:
        y = jnp.dot(a_ref[...], wb_ref[...], preferred_element_type=F32)
    _epilogue(y, rest[:-2], o_ref, epilogue)


def _proj(a, w, *, n, col0, out_dtype, epilogue, tm, tn, w_transposed=False, row_vecs=(), tiles=(),
          name):
    m, k = a.shape
    assert n % tn == 0 and m % tm == 0
    if w_transposed:
        assert col0 % SUBLANES == 0
        w_spec = pl.BlockSpec((pl.Element(tn), pl.Element(k)),
                              lambda j, i: (pl.multiple_of(col0 + j * tn, SUBLANES), 0))
        wb_shape = (tn, k)
    else:
        assert col0 % tn == 0
        w_spec = pl.BlockSpec((k, tn), lambda j, i: (0, col0 // tn + j))
        wb_shape = (k, tn)
    in_specs = [pl.BlockSpec((tm, k), lambda j, i: (i, 0)), w_spec]
    args = [a, w]
    for t in tiles:
        in_specs.append(pl.BlockSpec((tm, tn), lambda j, i: (i, j)))
        args.append(t)
    for v in row_vecs:
        in_specs.append(pl.BlockSpec((1, tn), lambda j, i: (0, j)))
        args.append(v.reshape(1, n))
    return pl.pallas_call(
        functools.partial(_proj_kernel, epilogue=epilogue, w_transposed=w_transposed),
        grid=(n // tn, m // tm),
        in_specs=in_specs,
        out_specs=pl.BlockSpec((tm, tn), lambda j, i: (i, j)),
        out_shape=jax.ShapeDtypeStruct((m, n), out_dtype),
        scratch_shapes=[pltpu.VMEM(wb_shape, BF16)],
        compiler_params=_params(("parallel", "arbitrary"), 56),
        name=name,
    )(*args)


def _cumsum_kernel(x_ref, ft_ref, *, chunk):
    s = x_ref.shape[0]
    row = lax.broadcasted_iota(jnp.int32, (chunk, LANES), 0)

    def body(c, carry):
        s0 = pl.multiple_of(c * chunk, chunk)
        x = x_ref[pl.ds(s0, chunk), :]
        d = 1
        while d < chunk:
            x = x + jnp.where(row >= d, pltpu.roll(x, d, 0), 0.0)
            d *= 2
        x = x + carry
        ft_ref[:, pl.ds(s0, chunk)] = x.T
        return x[chunk - 1:chunk, :]

    lax.fori_loop(0, s // chunk, body, jnp.zeros((1, LANES), F32))


def _cumsum_t(logf, *, chunk=512):
    s = logf.shape[0]
    return pl.pallas_call(
        functools.partial(_cumsum_kernel, chunk=chunk),
        out_shape=jax.ShapeDtypeStruct((LANES, s), F32),
        compiler_params=_params(None, 40),
        name="cumsum",
    )(logf)


N_SPLIT = 3


def _decay_features(ft_ref, qf_ref, kf_ref, *, chunk=512):
    s = ft_ref.shape[1]
    row = lax.broadcasted_iota(jnp.int32, (LANES, chunk), 0)

    def body(c, carry):
        c0 = pl.multiple_of(c * chunk, chunk)
        rem = ft_ref[:, pl.ds(c0, chunk)] * LOG2E
        qf = jnp.where((row >= N_SPLIT) & (row < 2 * N_SPLIT), 1.0, 0.0)
        kf = jnp.where(row < N_SPLIT, 1.0, 0.0)
        for p in range(N_SPLIT):
            piece = rem.astype(BF16).astype(F32)
            rem = rem - piece
            qf = jnp.where(row == p, piece, qf)
            kf = jnp.where(row == N_SPLIT + p, -piece, kf)
        qf_ref[pl.ds(c0, chunk), :] = qf.T.astype(BF16)
        kf_ref[pl.ds(c0, chunk), :] = kf.T.astype(BF16)
        return carry

    lax.fori_loop(0, s // chunk, body, 0)


def _attn_kernel(q_ref, k_ref, v_ref, ft_ref, z_ref, o_ref, qf_ref, kf_ref, va_ref, qa_ref, sa_ref,
                 sb_ref, m_ref, acc_ref, *, tq, tk):
    i = pl.program_id(1)
    reps = tk // LANES

    @pl.when(i == 0)
    def _():
        _decay_features(ft_ref, qf_ref, kf_ref)
        lane = lax.broadcasted_iota(jnp.int32, (tk, LANES), 1)
        ones_col = jnp.where(lane == 0, 1.0, 0.0).astype(BF16)

        def body(c, carry):
            c0 = pl.multiple_of(c * tk, tk)
            va_ref[pl.ds(c0, tk), :HEAD_DIM] = v_ref[pl.ds(c0, tk), :]
            va_ref[pl.ds(c0, tk), HEAD_DIM:] = ones_col
            return carry

        lax.fori_loop(0, v_ref.shape[0] // tk, body, 0)

    q0 = pl.multiple_of(i * tq, tq)
    qa_ref[:, :HEAD_DIM] = q_ref[...]
    qa_ref[:, HEAD_DIM:] = qf_ref[pl.ds(q0, tq), :]

    m_ref[...] = jnp.full(m_ref.shape, MASK_VALUE, F32)
    acc_ref[...] = jnp.zeros(acc_ref.shape, F32)

    def scores(j, dst_ref, r0=0):
        k0 = pl.multiple_of(j * tk, tk)
        ka = jnp.concatenate([k_ref[pl.ds(k0, tk), :], kf_ref[pl.ds(k0, tk), :]], axis=1)
        dst_ref[r0:, :] = lax.dot_general(qa_ref[r0:, :], ka, (((1,), (1,)), ((), ())),
                                          preferred_element_type=F32)

    def consume(src_ref, j, r0=0, causal=False):
        k0 = pl.multiple_of(j * tk, tk)
        nr = tq - r0
        s2 = src_ref[r0:, :]
        if causal:
            rows = lax.broadcasted_iota(jnp.int32, (nr, tk), 0)
            cols = lax.broadcasted_iota(jnp.int32, (nr, tk), 1)
            s2 = jnp.where(rows >= cols, s2, MASK_VALUE)
        m_prev = m_ref[r0:, :]
        m_new = jnp.maximum(m_prev, jnp.max(s2, axis=1, keepdims=True))
        alpha = jnp.exp2(m_prev - m_new)
        p = jnp.exp2((s2 - jnp.concatenate([m_new] * reps, axis=1)).astype(BF16))
        pv = jnp.dot(p, va_ref[pl.ds(k0, tk), :], preferred_element_type=F32)
        acc_ref[r0:, :] = jnp.concatenate([alpha, alpha], axis=1) * acc_ref[r0:, :] + pv
        m_ref[r0:, :] = m_new

    nk = tq // tk
    scores(0, sa_ref)

    def pair(jj, carry):
        j = 2 * jj
        scores(j + 1, sb_ref)
        consume(sa_ref, j)
        scores(j + 2, sa_ref)
        consume(sb_ref, j + 1)
        return carry

    lax.fori_loop(0, i * (nk // 2), pair, 0)
    bufs = (sa_ref, sb_ref)
    for d in range(nk):
        if d + 1 < nk:
            scores(nk * i + d + 1, bufs[(d + 1) % 2], r0=(d + 1) * tk)
        consume(bufs[d % 2], nk * i + d, r0=d * tk, causal=True)

    acc = acc_ref[...]
    att = acc[:, :HEAD_DIM] / acc[:, HEAD_DIM:HEAD_DIM + 1]
    o_ref[...] = (att * z_ref[...].astype(F32)).astype(o_ref.dtype)


def _attention(qk, v, z, ft, *, heads, tq=1024, tk=512):
    s = qk.shape[0]
    assert tq % (2 * tk) == 0 and s % tq == 0
    return pl.pallas_call(
        functools.partial(_attn_kernel, tq=tq, tk=tk),
        grid=(heads, s // tq),
        in_specs=[pl.BlockSpec((tq, HEAD_DIM), lambda h, i: (i, h)),
                  pl.BlockSpec((s, HEAD_DIM), lambda h, i: (0, heads + h)),
                  pl.BlockSpec((s, HEAD_DIM), lambda h, i: (0, h)),
                  pl.BlockSpec((None, 1, s), lambda h, i: (h, 0, 0)),
                  pl.BlockSpec((tq, HEAD_DIM), lambda h, i: (i, h))],
        out_specs=pl.BlockSpec((tq, HEAD_DIM), lambda h, i: (i, h)),
        out_shape=jax.ShapeDtypeStruct((s, heads * HEAD_DIM), BF16),
        scratch_shapes=[pltpu.VMEM((s, LANES), BF16),
                        pltpu.VMEM((s, LANES), BF16),
                        pltpu.VMEM((s, HEAD_DIM + LANES), BF16),
                        pltpu.VMEM((tq, HEAD_DIM + LANES), BF16),
                        pltpu.VMEM((tq, tk), F32),
                        pltpu.VMEM((tq, tk), F32),
                        pltpu.VMEM((tq, LANES), F32),
                        pltpu.VMEM((tq, HEAD_DIM + LANES), F32)],
        compiler_params=_params(("parallel", "arbitrary"), 48),
        name="attn",
    )(qk, qk, v, ft, z)


def _rnn_kernel(x_ref, z_ref, cw_ref, cb_ref, wax_ref, ba_ref, bx_ref, lam_ref, o_ref,
                xprev_ref, hc_ref, a_ref, b_ref, *, tt, wt):
    t = pl.program_id(1)
    groups = tt // SUBLANES

    @pl.when(t == 0)
    def _():
        xprev_ref[...] = jnp.zeros(xprev_ref.shape, F32)
        hc_ref[...] = jnp.zeros(hc_ref.shape, F32)

    x = x_ref[...]
    xp = xprev_ref[...]
    row8 = lax.broadcasted_iota(jnp.int32, (SUBLANES, wt), 0)
    xc = cb_ref[...] + x * cw_ref[CONV_W - 1:CONV_W, :]
    for j in range(1, CONV_W):
        r = pltpu.roll(x, j, 0)
        head = jnp.where(row8 < j, pltpu.roll(xp, j, 0), r[:SUBLANES])
        xs = jnp.concatenate([head, r[SUBLANES:]], axis=0)
        xc = xc + xs * cw_ref[CONV_W - 1 - j:CONV_W - j, :]
    xprev_ref[...] = x[tt - SUBLANES:, :]

    xcb = xc.astype(BF16)
    ga, gx = [], []
    for n in range(wt // LANES):
        g = jnp.dot(xcb[:, n * LANES:(n + 1) * LANES], wax_ref[n], preferred_element_type=F32)
        ga.append(g[:, :LANES])
        gx.append(g[:, LANES:])
    r_gate = _sigmoid(jnp.concatenate(ga, axis=1) + ba_ref[...])
    i_gate = _sigmoid(jnp.concatenate(gx, axis=1) + bx_ref[...])
    log_a = (-RG_C) * r_gate * _softplus(-lam_ref[...])
    a = jnp.exp(log_a)
    mult = jnp.sqrt(-jnp.tanh(log_a) * (a * a + 1.0))
    b = mult * (i_gate * xc)

    a3 = a.reshape(groups, SUBLANES, wt)
    b3 = b.reshape(groups, SUBLANES, wt)
    sub = lax.broadcasted_iota(jnp.int32, (groups, SUBLANES, wt), 1)
    d = 1
    while d < SUBLANES:
        keep = sub >= d
        b3 = a3 * jnp.where(keep, pltpu.roll(b3, d, 1), 0.0) + b3
        a3 = a3 * jnp.where(keep, pltpu.roll(a3, d, 1), 1.0)
        d *= 2
    a_ref[...] = a3.reshape(tt, wt)
    b_ref[...] = b3.reshape(tt, wt)

    def body(g, hprev):
        g0 = pl.multiple_of(g * SUBLANES, SUBLANES)
        hg = b_ref[pl.ds(g0, SUBLANES), :] + a_ref[pl.ds(g0, SUBLANES), :] * hprev
        b_ref[pl.ds(g0, SUBLANES), :] = hg
        return jnp.broadcast_to(hg[SUBLANES - 1:SUBLANES, :], (SUBLANES, wt))

    hlast = lax.fori_loop(0, groups, body, hc_ref[...], unroll=8)
    hc_ref[...] = hlast
    o_ref[...] = (b_ref[...] * z_ref[...].astype(F32)).astype(o_ref.dtype)


def _rnn(x_r, z_r, conv_w, conv_b, w_ax, b_a, b_x, lam, *, tt=1024, wt=512):
    s, w = x_r.shape
    nb = wt // LANES
    tile = pl.BlockSpec((tt, wt), lambda n, t: (t, n))
    vec = pl.BlockSpec((1, wt), lambda n, t: (0, n))
    return pl.pallas_call(
        functools.partial(_rnn_kernel, tt=tt, wt=wt),
        grid=(w // wt, s // tt),
        in_specs=[tile, tile,
                  pl.BlockSpec((CONV_W, wt), lambda n, t: (0, n)), vec,
                  pl.BlockSpec((nb, LANES, 2 * LANES), lambda n, t: (n, 0, 0)),
                  vec, vec, vec],
        out_specs=tile,
        out_shape=jax.ShapeDtypeStruct((s, w), BF16),
        scratch_shapes=[pltpu.VMEM((SUBLANES, wt), F32),
                        pltpu.VMEM((SUBLANES, wt), F32),
                        pltpu.VMEM((tt, wt), F32),
                        pltpu.VMEM((tt, wt), F32)],
        compiler_params=_params(("parallel", "arbitrary"), 40),
        name="rnn",
    )(x_r, z_r, conv_w, conv_b.reshape(1, w), w_ax, b_a.reshape(1, w), b_x.reshape(1, w),
      lam.reshape(1, w))


def _merge_kernel(ya_ref, yr_ref, ga_ref, gr_ref, wba_ref, wbr_ref, o_ref, wa_ref, wr_ref):
    @pl.when(pl.program_id(1) == 0)
    def _():
        _cast_weight_tile(wba_ref, wa_ref)
        _cast_weight_tile(wbr_ref, wr_ref)

    y_a = jnp.dot(ya_ref[...], wa_ref[...], preferred_element_type=F32)
    acc = ga_ref[...].astype(F32) * y_a
    y_r = jnp.dot(yr_ref[...], wr_ref[...], preferred_element_type=F32)
    o_ref[...] = (acc + gr_ref[...].astype(F32) * y_r).astype(o_ref.dtype)


def _merge(ya, yr, g, w_ba, w_br, *, tm=1024, tn=512):
    s, wd = ya.shape
    d = w_ba.shape[1]
    nj = d // tn
    return pl.pallas_call(
        _merge_kernel,
        grid=(d // tn, s // tm),
        in_specs=[pl.BlockSpec((tm, wd), lambda j, i: (i, 0)),
                  pl.BlockSpec((tm, wd), lambda j, i: (i, 0)),
                  pl.BlockSpec((tm, tn), lambda j, i: (i, j)),
                  pl.BlockSpec((tm, tn), lambda j, i: (i, nj + j)),
                  pl.BlockSpec((wd, tn), lambda j, i: (0, j)),
                  pl.BlockSpec((wd, tn), lambda j, i: (0, j))],
        out_specs=pl.BlockSpec((tm, tn), lambda j, i: (i, j)),
        out_shape=jax.ShapeDtypeStruct((s, d), BF16),
        scratch_shapes=[pltpu.VMEM((wd, tn), BF16), pltpu.VMEM((wd, tn), BF16)],
        compiler_params=_params(("parallel", "arbitrary"), 56),
        name="merge",
    )(ya, yr, g, g, w_ba, w_br)


def kernel(x, c, w_ada, b_ada, norm_w, w_in, b_f, q_norm_w, k_norm_w, conv_w, conv_b, w_rg_a,
           b_rg_a, w_rg_x, b_rg_x, lru_lambda, w_br_attn, w_br_rnn, w_gate, b_gate, w_out):
    bsz, s, d = x.shape
    assert bsz == 1
    aw = w_br_attn.shape[0]
    rw = w_br_rnn.shape[0]
    heads = aw // HEAD_DIM
    x2 = x.reshape(s, d)

    mod = _ada(c, w_ada, b_ada)
    shift, scale, gate_res = mod[:, :d], mod[:, d:2 * d], mod[:, 2 * d:]
    h = _prenorm(x2, norm_w, scale, shift)

    o_f = 4 * aw
    o_xr = o_f + heads
    qk_gain = jnp.concatenate([jnp.tile(q_norm_w * (LOG2E / math.sqrt(HEAD_DIM)), heads),
                               jnp.tile(k_norm_w, heads)])
    proj = functools.partial(_proj, h, w_in.T, w_transposed=True, tm=1024, tn=512)
    qk = proj(n=2 * aw, col0=0, out_dtype=BF16, epilogue="headnorm", row_vecs=(qk_gain,),
              name="proj_qk")
    v = proj(n=aw, col0=2 * aw, out_dtype=BF16, epilogue="plain", name="proj_v")
    z_a = proj(n=aw, col0=3 * aw, out_dtype=BF16, epilogue="silu", name="proj_za")
    logf = _proj(h, w_in.T, w_transposed=True, n=LANES, col0=o_f, out_dtype=F32,
                 epilogue="logsigmoid", tm=1024, tn=LANES,
                 row_vecs=(jnp.pad(b_f, (0, LANES - heads)),), name="proj_f")
    x_r = proj(n=rw, col0=o_xr, out_dtype=F32, epilogue="plain", name="proj_xr")
    z_r = proj(n=rw, col0=o_xr + rw, out_dtype=BF16, epilogue="silu", name="proj_zr")

    ft = _cumsum_t(logf)[:heads].reshape(heads, 1, s)
    ya = _attention(qk, v, z_a, ft, heads=heads)

    w_ax = jnp.concatenate([w_rg_a, w_rg_x], axis=2).astype(BF16)
    yr = _rnn(x_r, z_r, conv_w, conv_b, w_ax, b_rg_a, b_rg_x, lru_lambda)

    g = _proj(h, w_gate, n=2 * d, col0=0, out_dtype=BF16, epilogue="sigmoid_bias", tm=1024, tn=512,
              row_vecs=(b_gate,), name="gate")
    merged = _merge(ya, yr, g, w_br_attn, w_br_rnn)
    y = _proj(merged, w_out, n=d, col0=0, out_dtype=x.dtype, epilogue="residual", tm=1024, tn=512,
              tiles=(x2,), row_vecs=(gate_res.reshape(-1),), name="out")
    return y.reshape(bsz, s, d)
```

```python
import functools
import math

import jax
import jax.numpy as jnp
from jax import lax
from jax.experimental import pallas as pl
from jax.experimental.pallas import tpu as pltpu

F32 = jnp.float32
BF16 = jnp.bfloat16

LANES = 128
SUBLANES = 8
HEAD_DIM = 128
CONV_W = 4
RG_C = 8.0
EPS = 1e-6
MASK_VALUE = -1e30
LOG2E = 1.4426950408889634
MIB = 1024 * 1024


def _params(semantics, vmem_mib):
    return pltpu.CompilerParams(dimension_semantics=semantics,
                                vmem_limit_bytes=vmem_mib * MIB)


def _sigmoid(x):
    return 1.0 / (1.0 + jnp.exp(-x))


def _silu(x):
    return x * _sigmoid(x)


def _softplus(x):
    return jnp.maximum(x, 0.0) + jnp.log1p(jnp.exp(-jnp.abs(x)))


def _ada_kernel(c_ref, w_ref, b_ref, o_ref, *, rows):
    d = w_ref.shape[0]

    def body(r, acc):
        r0 = pl.multiple_of(r * rows, rows)
        c = c_ref[pl.ds(r0, rows), :]
        return acc + jnp.sum(w_ref[pl.ds(r0, rows), :] * _silu(c), axis=0, keepdims=True)

    acc = lax.fori_loop(0, d // rows, body, jnp.zeros(o_ref.shape, F32))
    o_ref[...] = acc + b_ref[...]


def _ada(c, w_ada, b_ada, *, tn=512, rows=512):
    d, n = w_ada.shape
    return pl.pallas_call(
        functools.partial(_ada_kernel, rows=rows),
        grid=(n // tn,),
        in_specs=[pl.BlockSpec((d, 1), lambda j: (0, 0)),
                  pl.BlockSpec((d, tn), lambda j: (0, j)),
                  pl.BlockSpec((1, tn), lambda j: (0, j))],
        out_specs=pl.BlockSpec((1, tn), lambda j: (0, j)),
        out_shape=jax.ShapeDtypeStruct((1, n), F32),
        compiler_params=_params(("parallel",), 40),
        name="ada",
    )(c.reshape(d, 1), w_ada, b_ada.reshape(1, n))


def _prenorm_kernel(x_ref, nw_ref, scale_ref, shift_ref, o_ref):
    x = x_ref[...]
    ms = jnp.mean(x * x, axis=-1, keepdims=True)
    h = x * lax.rsqrt(ms + EPS) * nw_ref[...]
    o_ref[...] = (h * (1.0 + scale_ref[...]) + shift_ref[...]).astype(o_ref.dtype)


def _prenorm(x, norm_w, scale, shift, *, tm=256):
    s, d = x.shape
    vec = pl.BlockSpec((1, d), lambda i: (0, 0))
    return pl.pallas_call(
        _prenorm_kernel,
        grid=(s // tm,),
        in_specs=[pl.BlockSpec((tm, d), lambda i: (i, 0)), vec, vec, vec],
        out_specs=pl.BlockSpec((tm, d), lambda i: (i, 0)),
        out_shape=jax.ShapeDtypeStruct((s, d), BF16),
        compiler_params=_params(("parallel",), 40),
        name="prenorm",
    )(x, norm_w.reshape(1, d), scale, shift)


def _cast_weight_tile(w_ref, wb_ref, *, rows=128):
    n_rows = w_ref.shape[0]

    def body(r, carry):
        r0 = pl.multiple_of(r * rows, rows)
        wb_ref[pl.ds(r0, rows), :] = w_ref[pl.ds(r0, rows), :].astype(BF16)
        return carry

    lax.fori_loop(0, n_rows // rows, body, 0)


def _epilogue(y, rest, o_ref, epilogue):
    if epilogue == "plain":
        o_ref[...] = y.astype(o_ref.dtype)
    elif epilogue == "silu":
        o_ref[...] = _silu(y).astype(o_ref.dtype)
    elif epilogue == "sigmoid_bias":
        o_ref[...] = _sigmoid(y + rest[0][...]).astype(o_ref.dtype)
    elif epilogue == "headnorm":
        gain_ref = rest[0]
        for hh in range(y.shape[1] // HEAD_DIM):
            sl = slice(hh * HEAD_DIM, (hh + 1) * HEAD_DIM)
            yh = y[:, sl]
            ms = jnp.mean(yh * yh, axis=-1, keepdims=True)
            o_ref[:, sl] = (yh * lax.rsqrt(ms + EPS) * gain_ref[:, sl]).astype(o_ref.dtype)
    elif epilogue == "logsigmoid":
        z = y + rest[0][...]
        o_ref[...] = (jnp.minimum(z, 0.0) - jnp.log1p(jnp.exp(-jnp.abs(z)))).astype(o_ref.dtype)
    elif epilogue == "residual":
        x_ref, g_ref = rest[0], rest[1]
        o_ref[...] = (x_ref[...] + g_ref[...] * y).astype(o_ref.dtype)
    else:
        raise ValueError(epilogue)


def _proj_kernel(a_ref, w_ref, *rest, epilogue, w_transposed):
    wb_ref = rest[-1]
    o_ref = rest[-2]

    @pl.when(pl.program_id(1) == 0)
    def _():
        _cast_weight_tile(w_ref, wb_ref)

    if w_transposed:
        y = lax.dot_general(a_ref[...], wb_ref[...], (((1,), (1,)), ((), ())),
                            preferred_element_type=F32)
    else:
        y = jnp.dot(a_ref[...], wb_ref[...], preferred_element_type=F32)
    _epilogue(y, rest[:-2], o_ref, epilogue)


def _proj(a, w, *, n, col0, out_dtype, epilogue, tm, tn, w_transposed=False, row_vecs=(), tiles=(),
          name):
    m, k = a.shape
    assert n % tn == 0 and m % tm == 0
    if w_transposed:
        assert col0 % SUBLANES == 0
        w_spec = pl.BlockSpec((pl.Element(tn), pl.Element(k)),
                              lambda j, i: (pl.multiple_of(col0 + j * tn, SUBLANES), 0))
        wb_shape = (tn, k)
    else:
        assert col0 % tn == 0
        w_spec = pl.BlockSpec((k, tn), lambda j, i: (0, col0 // tn + j))
        wb_shape = (k, tn)
    in_specs = [pl.BlockSpec((tm, k), lambda j, i: (i, 0)), w_spec]
    args = [a, w]
    for t in tiles:
        in_specs.append(pl.BlockSpec((tm, tn), lambda j, i: (i, j)))
        args.append(t)
    for v in row_vecs:
        in_specs.append(pl.BlockSpec((1, tn), lambda j, i: (0, j)))
        args.append(v.reshape(1, n))
    return pl.pallas_call(
        functools.partial(_proj_kernel, epilogue=epilogue, w_transposed=w_transposed),
        grid=(n // tn, m // tm),
        in_specs=in_specs,
        out_specs=pl.BlockSpec((tm, tn), lambda j, i: (i, j)),
        out_shape=jax.ShapeDtypeStruct((m, n), out_dtype),
        scratch_shapes=[pltpu.VMEM(wb_shape, BF16)],
        compiler_params=_params(("parallel", "arbitrary"), 56),
        name=name,
    )(*args)


def _cumsum_kernel(x_ref, ft_ref, *, chunk):
    s = x_ref.shape[0]
    row = lax.broadcasted_iota(jnp.int32, (chunk, LANES), 0)

    def body(c, carry):
        s0 = pl.multiple_of(c * chunk, chunk)
        x = x_ref[pl.ds(s0, chunk), :]
        d = 1
        while d < chunk:
            x = x + jnp.where(row >= d, pltpu.roll(x, d, 0), 0.0)
            d *= 2
        x = x + carry
        ft_ref[:, pl.ds(s0, chunk)] = x.T
        return x[chunk - 1:chunk, :]

    lax.fori_loop(0, s // chunk, body, jnp.zeros((1, LANES), F32))


def _cumsum_t(logf, *, chunk=512):
    s = logf.shape[0]
    return pl.pallas_call(
        functools.partial(_cumsum_kernel, chunk=chunk),
        out_shape=jax.ShapeDtypeStruct((LANES, s), F32),
        compiler_params=_params(None, 40),
        name="cumsum",
    )(logf)


N_SPLIT = 3


def _decay_features(ft_ref, qf_ref, kf_ref, *, chunk=512):
    s = ft_ref.shape[1]
    row = lax.broadcasted_iota(jnp.int32, (LANES, chunk), 0)

    def body(c, carry):
        c0 = pl.multiple_of(c * chunk, chunk)
        rem = ft_ref[:, pl.ds(c0, chunk)] * LOG2E
        qf = jnp.where((row >= N_SPLIT) & (row < 2 * N_SPLIT), 1.0, 0.0)
        kf = jnp.where(row < N_SPLIT, 1.0, 0.0)
        for p in range(N_SPLIT):
            piece = rem.astype(BF16).astype(F32)
            rem = rem - piece
            qf = jnp.where(row == p, piece, qf)
            kf = jnp.where(row == N_SPLIT + p, -piece, kf)
        qf_ref[pl.ds(c0, chunk), :] = qf.T.astype(BF16)
        kf_ref[pl.ds(c0, chunk), :] = kf.T.astype(BF16)
        return carry

    lax.fori_loop(0, s // chunk, body, 0)


def _attn_kernel(q_ref, k_ref, v_ref, ft_ref, z_ref, o_ref, qf_ref, kf_ref, va_ref, qa_ref, sa_ref,
                 sb_ref, m_ref, acc_ref, *, tq, tk):
    i = pl.program_id(1)
    reps = tk // LANES

    @pl.when(i == 0)
    def _():
        _decay_features(ft_ref, qf_ref, kf_ref)
        lane = lax.broadcasted_iota(jnp.int32, (tk, LANES), 1)
        ones_col = jnp.where(lane == 0, 1.0, 0.0).astype(BF16)

        def body(c, carry):
            c0 = pl.multiple_of(c * tk, tk)
            va_ref[pl.ds(c0, tk), :HEAD_DIM] = v_ref[pl.ds(c0, tk), :]
            va_ref[pl.ds(c0, tk), HEAD_DIM:] = ones_col
            return carry

        lax.fori_loop(0, v_ref.shape[0] // tk, body, 0)

    q0 = pl.multiple_of(i * tq, tq)
    qa_ref[:, :HEAD_DIM] = q_ref[...]
    qa_ref[:, HEAD_DIM:] = qf_ref[pl.ds(q0, tq), :]

    m_ref[...] = jnp.full(m_ref.shape, MASK_VALUE, F32)
    acc_ref[...] = jnp.zeros(acc_ref.shape, F32)

    def scores(j, dst_ref, r0=0):
        k0 = pl.multiple_of(j * tk, tk)
        ka = jnp.concatenate([k_ref[pl.ds(k0, tk), :], kf_ref[pl.ds(k0, tk), :]], axis=1)
        dst_ref[r0:, :] = lax.dot_general(qa_ref[r0:, :], ka, (((1,), (1,)), ((), ())),
                                          preferred_element_type=F32)

    def consume(src_ref, j, r0=0, causal=False):
        k0 = pl.multiple_of(j * tk, tk)
        nr = tq - r0
        s2 = src_ref[r0:, :]
        if causal:
            rows = lax.broadcasted_iota(jnp.int32, (nr, tk), 0)
            cols = lax.broadcasted_iota(jnp.int32, (nr, tk), 1)
            s2 = jnp.where(rows >= cols, s2, MASK_VALUE)
        m_prev = m_ref[r0:, :]
        m_new = jnp.maximum(m_prev, jnp.max(s2, axis=1, keepdims=True))
        alpha = jnp.exp2(m_prev - m_new)
        p = jnp.exp2((s2 - jnp.concatenate([m_new] * reps, axis=1)).astype(BF16))
        pv = jnp.dot(p, va_ref[pl.ds(k0, tk), :], preferred_element_type=F32)
        acc_ref[r0:, :] = jnp.concatenate([alpha, alpha], axis=1) * acc_ref[r0:, :] + pv
        m_ref[r0:, :] = m_new

    nk = tq // tk
    scores(0, sa_ref)

    def pair(jj, carry):
        j = 2 * jj
        scores(j + 1, sb_ref)
        consume(sa_ref, j)
        scores(j + 2, sa_ref)
        consume(sb_ref, j + 1)
        return carry

    lax.fori_loop(0, i * (nk // 2), pair, 0)
    bufs = (sa_ref, sb_ref)
    for d in range(nk):
        if d + 1 < nk:
            scores(nk * i + d + 1, bufs[(d + 1) % 2], r0=(d + 1) * tk)
        consume(bufs[d % 2], nk * i + d, r0=d * tk, causal=True)

    acc = acc_ref[...]
    att = acc[:, :HEAD_DIM] / acc[:, HEAD_DIM:HEAD_DIM + 1]
    o_ref[...] = (att * z_ref[...].astype(F32)).astype(o_ref.dtype)


def _attention(qk, v, z, ft, *, heads, tq=1024, tk=512):
    s = qk.shape[0]
    assert tq % (2 * tk) == 0 and s % tq == 0
    return pl.pallas_call(
        functools.partial(_attn_kernel, tq=tq, tk=tk),
        grid=(heads, s // tq),
        in_specs=[pl.BlockSpec((tq, HEAD_DIM), lambda h, i: (i, h)),
                  pl.BlockSpec((s, HEAD_DIM), lambda h, i: (0, heads + h)),
                  pl.BlockSpec((s, HEAD_DIM), lambda h, i: (0, h)),
                  pl.BlockSpec((None, 1, s), lambda h, i: (h, 0, 0)),
                  pl.BlockSpec((tq, HEAD_DIM), lambda h, i: (i, h))],
        out_specs=pl.BlockSpec((tq, HEAD_DIM), lambda h, i: (i, h)),
        out_shape=jax.ShapeDtypeStruct((s, heads * HEAD_DIM), BF16),
        scratch_shapes=[pltpu.VMEM((s, LANES), BF16),
                        pltpu.VMEM((s, LANES), BF16),
                        pltpu.VMEM((s, HEAD_DIM + LANES), BF16),
                        pltpu.VMEM((tq, HEAD_DIM + LANES), BF16),
                        pltpu.VMEM((tq, tk), F32),
                        pltpu.VMEM((tq, tk), F32),
                        pltpu.VMEM((tq, LANES), F32),
                        pltpu.VMEM((tq, HEAD_DIM + LANES), F32)],
        compiler_params=_params(("parallel", "arbitrary"), 48),
        name="attn",
    )(qk, qk, v, ft, z)


def _rnn_kernel(x_ref, z_ref, cw_ref, cb_ref, wax_ref, ba_ref, bx_ref, lam_ref, o_ref,
                xprev_ref, hc_ref, a_ref, b_ref, *, tt, wt):
    t = pl.program_id(1)
    groups = tt // SUBLANES

    @pl.when(t == 0)
    def _():
        xprev_ref[...] = jnp.zeros(xprev_ref.shape, F32)
        hc_ref[...] = jnp.zeros(hc_ref.shape, F32)

    x = x_ref[...]
    xp = xprev_ref[...]
    row8 = lax.broadcasted_iota(jnp.int32, (SUBLANES, wt), 0)
    xc = cb_ref[...] + x * cw_ref[CONV_W - 1:CONV_W, :]
    for j in range(1, CONV_W):
        r = pltpu.roll(x, j, 0)
        head = jnp.where(row8 < j, pltpu.roll(xp, j, 0), r[:SUBLANES])
        xs = jnp.concatenate([head, r[SUBLANES:]], axis=0)
        xc = xc + xs * cw_ref[CONV_W - 1 - j:CONV_W - j, :]
    xprev_ref[...] = x[tt - SUBLANES:, :]

    xcb = xc.astype(BF16)
    ga, gx = [], []
    for n in range(wt // LANES):
        g = jnp.dot(xcb[:, n * LANES:(n + 1) * LANES], wax_ref[n], preferred_element_type=F32)
        ga.append(g[:, :LANES])
        gx.append(g[:, LANES:])
    r_gate = _sigmoid(jnp.concatenate(ga, axis=1) + ba_ref[...])
    i_gate = _sigmoid(jnp.concatenate(gx, axis=1) + bx_ref[...])
    log_a = (-RG_C) * r_gate * _softplus(-lam_ref[...])
    a = jnp.exp(log_a)
    mult = jnp.sqrt(-jnp.tanh(log_a) * (a * a + 1.0))
    b = mult * (i_gate * xc)

    a3 = a.reshape(groups, SUBLANES, wt)
    b3 = b.reshape(groups, SUBLANES, wt)
    sub = lax.broadcasted_iota(jnp.int32, (groups, SUBLANES, wt), 1)
    d = 1
    while d < SUBLANES:
        keep = sub >= d
        b3 = a3 * jnp.where(keep, pltpu.roll(b3, d, 1), 0.0) + b3
        a3 = a3 * jnp.where(keep, pltpu.roll(a3, d, 1), 1.0)
        d *= 2
    a_ref[...] = a3.reshape(tt, wt)
    b_ref[...] = b3.reshape(tt, wt)

    def body(g, hprev):
        g0 = pl.multiple_of(g * SUBLANES, SUBLANES)
        hg = b_ref[pl.ds(g0, SUBLANES), :] + a_ref[pl.ds(g0, SUBLANES), :] * hprev
        b_ref[pl.ds(g0, SUBLANES), :] = hg
        return jnp.broadcast_to(hg[SUBLANES - 1:SUBLANES, :], (SUBLANES, wt))

    hlast = lax.fori_loop(0, groups, body, hc_ref[...], unroll=8)
    hc_ref[...] = hlast
    o_ref[...] = (b_ref[...] * z_ref[...].astype(F32)).astype(o_ref.dtype)


def _rnn(x_r, z_r, conv_w, conv_b, w_ax, b_a, b_x, lam, *, tt=1024, wt=512):
    s, w = x_r.shape
    nb = wt // LANES
    tile = pl.BlockSpec((tt, wt), lambda n, t: (t, n))
    vec = pl.BlockSpec((1, wt), lambda n, t: (0, n))
    return pl.pallas_call(
        functools.partial(_rnn_kernel, tt=tt, wt=wt),
        grid=(w // wt, s // tt),
        in_specs=[tile, tile,
                  pl.BlockSpec((CONV_W, wt), lambda n, t: (0, n)), vec,
                  pl.BlockSpec((nb, LANES, 2 * LANES), lambda n, t: (n, 0, 0)),
                  vec, vec, vec],
        out_specs=tile,
        out_shape=jax.ShapeDtypeStruct((s, w), BF16),
        scratch_shapes=[pltpu.VMEM((SUBLANES, wt), F32),
                        pltpu.VMEM((SUBLANES, wt), F32),
                        pltpu.VMEM((tt, wt), F32),
                        pltpu.VMEM((tt, wt), F32)],
        compiler_params=_params(("parallel", "arbitrary"), 40),
        name="rnn",
    )(x_r, z_r, conv_w, conv_b.reshape(1, w), w_ax, b_a.reshape(1, w), b_x.reshape(1, w),
      lam.reshape(1, w))


def _merge_kernel(ya_ref, yr_ref, ga_ref, gr_ref, wba_ref, wbr_ref, o_ref, wa_ref, wr_ref):
    @pl.when(pl.program_id(1) == 0)
    def _():
        _cast_weight_tile(wba_ref, wa_ref)
        _cast_weight_tile(wbr_ref, wr_ref)

    y_a = jnp.dot(ya_ref[...], wa_ref[...], preferred_element_type=F32)
    acc = ga_ref[...].astype(F32) * y_a
    y_r = jnp.dot(yr_ref[...], wr_ref[...], preferred_element_type=F32)
    o_ref[...] = (acc + gr_ref[...].astype(F32) * y_r).astype(o_ref.dtype)


def _merge(ya, yr, g, w_ba, w_br, *, tm=1024, tn=512):
    s, wd = ya.shape
    d = w_ba.shape[1]
    nj = d // tn
    return pl.pallas_call(
        _merge_kernel,
        grid=(d // tn, s // tm),
        in_specs=[pl.BlockSpec((tm, wd), lambda j, i: (i, 0)),
                  pl.BlockSpec((tm, wd), lambda j, i: (i, 0)),
                  pl.BlockSpec((tm, tn), lambda j, i: (i, j)),
                  pl.BlockSpec((tm, tn), lambda j, i: (i, nj + j)),
                  pl.BlockSpec((wd, tn), lambda j, i: (0, j)),
                  pl.BlockSpec((wd, tn), lambda j, i: (0, j))],
        out_specs=pl.BlockSpec((tm, tn), lambda j, i: (i, j)),
        out_shape=jax.ShapeDtypeStruct((s, d), BF16),
        scratch_shapes=[pltpu.VMEM((wd, tn), BF16), pltpu.VMEM((wd, tn), BF16)],
        compiler_params=_params(("parallel", "arbitrary"), 56),
        name="merge",
    )(ya, yr, g, g, w_ba, w_br)


def kernel(x, c, w_ada, b_ada, norm_w, w_in, b_f, q_norm_w, k_norm_w, conv_w, conv_b, w_rg_a,
           b_rg_a, w_rg_x, b_rg_x, lru_lambda, w_br_attn, w_br_rnn, w_gate, b_gate, w_out):
    bsz, s, d = x.shape
    assert bsz == 1
    aw = w_br_attn.shape[0]
    rw = w_br_rnn.shape[0]
    heads = aw // HEAD_DIM
    x2 = x.reshape(s, d)

    mod = _ada(c, w_ada, b_ada)
    shift, scale, gate_res = mod[:, :d], mod[:, d:2 * d], mod[:, 2 * d:]
    h = _prenorm(x2, norm_w, scale, shift)

    o_f = 4 * aw
    o_xr = o_f + heads
    qk_gain = jnp.concatenate([jnp.tile(q_norm_w * (LOG2E / math.sqrt(HEAD_DIM)), heads),
                               jnp.tile(k_norm_w, heads)])
    proj = functools.partial(_proj, h, w_in.T, w_transposed=True, tm=512, tn=1024)
    qk = proj(n=2 * aw, col0=0, out_dtype=BF16, epilogue="headnorm", row_vecs=(qk_gain,),
              name="proj_qk")
    v = proj(n=aw, col0=2 * aw, out_dtype=BF16, epilogue="plain", name="proj_v")
    z_a = proj(n=aw, col0=3 * aw, out_dtype=BF16, epilogue="silu", name="proj_za")
    logf = _proj(h, w_in.T, w_transposed=True, n=LANES, col0=o_f, out_dtype=F32,
                 epilogue="logsigmoid", tm=1024, tn=LANES,
                 row_vecs=(jnp.pad(b_f, (0, LANES - heads)),), name="proj_f")
    x_r = proj(n=rw, col0=o_xr, out_dtype=F32, epilogue="plain", name="proj_xr")
    z_r = proj(n=rw, col0=o_xr + rw, out_dtype=BF16, epilogue="silu", name="proj_zr")

    ft = _cumsum_t(logf)[:heads].reshape(heads, 1, s)
    ya = _attention(qk, v, z_a, ft, heads=heads)

    w_ax = jnp.concatenate([w_rg_a, w_rg_x], axis=2).astype(BF16)
    yr = _rnn(x_r, z_r, conv_w, conv_b, w_ax, b_rg_a, b_rg_x, lru_lambda)

    g = _proj(h, w_gate, n=2 * d, col0=0, out_dtype=BF16, epilogue="sigmoid_bias", tm=512, tn=1024,
              row_vecs=(b_gate,), name="gate")
    merged = _merge(ya, yr, g, w_br_attn, w_br_rnn)
    y = _proj(merged, w_out, n=d, col0=0, out_dtype=x.dtype, epilogue="residual", tm=1024, tn=512,
              tiles=(x2,), row_vecs=(gate_res.reshape(-1),), name="out")
    return y.reshape(bsz, s, d)
```

```python
import functools
import math

import jax
import jax.numpy as jnp
from jax import lax
from jax.experimental import pallas as pl
from jax.experimental.pallas import tpu as pltpu

F32 = jnp.float32
BF16 = jnp.bfloat16

LANES = 128
SUBLANES = 8
HEAD_DIM = 128
CONV_W = 4
RG_C = 8.0
EPS = 1e-6
MASK_VALUE = -1e30
LOG2E = 1.4426950408889634
MIB = 1024 * 1024


def _params(semantics, vmem_mib):
    return pltpu.CompilerParams(dimension_semantics=semantics,
                                vmem_limit_bytes=vmem_mib * MIB)


def _sigmoid(x):
    return 1.0 / (1.0 + jnp.exp(-x))


def _silu(x):
    return x * _sigmoid(x)


def _softplus(x):
    return jnp.maximum(x, 0.0) + jnp.log1p(jnp.exp(-jnp.abs(x)))


def _ada_kernel(c_ref, w_ref, b_ref, o_ref, *, rows):
    d = w_ref.shape[0]

    def body(r, acc):
        r0 = pl.multiple_of(r * rows, rows)
        c = c_ref[pl.ds(r0, rows), :]
        return acc + jnp.sum(w_ref[pl.ds(r0, rows), :] * _silu(c), axis=0, keepdims=True)

    acc = lax.fori_loop(0, d // rows, body, jnp.zeros(o_ref.shape, F32))
    o_ref[...] = acc + b_ref[...]


def _ada(c, w_ada, b_ada, *, tn=512, rows=512):
    d, n = w_ada.shape
    return pl.pallas_call(
        functools.partial(_ada_kernel, rows=rows),
        grid=(n // tn,),
        in_specs=[pl.BlockSpec((d, 1), lambda j: (0, 0)),
                  pl.BlockSpec((d, tn), lambda j: (0, j)),
                  pl.BlockSpec((1, tn), lambda j: (0, j))],
        out_specs=pl.BlockSpec((1, tn), lambda j: (0, j)),
        out_shape=jax.ShapeDtypeStruct((1, n), F32),
        compiler_params=_params(("parallel",), 40),
        name="ada",
    )(c.reshape(d, 1), w_ada, b_ada.reshape(1, n))


def _prenorm_kernel(x_ref, nw_ref, scale_ref, shift_ref, o_ref):
    x = x_ref[...]
    ms = jnp.mean(x * x, axis=-1, keepdims=True)
    h = x * lax.rsqrt(ms + EPS) * nw_ref[...]
    o_ref[...] = (h * (1.0 + scale_ref[...]) + shift_ref[...]).astype(o_ref.dtype)


def _prenorm(x, norm_w, scale, shift, *, tm=512):
    s, d = x.shape
    vec = pl.BlockSpec((1, d), lambda i: (0, 0))
    return pl.pallas_call(
        _prenorm_kernel,
        grid=(s // tm,),
        in_specs=[pl.BlockSpec((tm, d), lambda i: (i, 0)), vec, vec, vec],
        out_specs=pl.BlockSpec((tm, d), lambda i: (i, 0)),
        out_shape=jax.ShapeDtypeStruct((s, d), BF16),
        compiler_params=_params(("parallel",), 40),
        name="prenorm",
    )(x, norm_w.reshape(1, d), scale, shift)


def _cast_weight_tile(w_ref, wb_ref, *, rows=128):
    n_rows = w_ref.shape[0]

    def body(r, carry):
        r0 = pl.multiple_of(r * rows, rows)
        wb_ref[pl.ds(r0, rows), :] = w_ref[pl.ds(r0, rows), :].astype(BF16)
        return carry

    lax.fori_loop(0, n_rows // rows, body, 0)


def _epilogue(y, rest, o_ref, epilogue):
    if epilogue == "plain":
        o_ref[...] = y.astype(o_ref.dtype)
    elif epilogue == "silu":
        o_ref[...] = _silu(y).astype(o_ref.dtype)
    elif epilogue == "sigmoid_bias":
        o_ref[...] = _sigmoid(y + rest[0][...]).astype(o_ref.dtype)
    elif epilogue == "headnorm":
        gain_ref = rest[0]
        for hh in range(y.shape[1] // HEAD_DIM):
            sl = slice(hh * HEAD_DIM, (hh + 1) * HEAD_DIM)
            yh = y[:, sl]
            ms = jnp.mean(yh * yh, axis=-1, keepdims=True)
            o_ref[:, sl] = (yh * lax.rsqrt(ms + EPS) * gain_ref[:, sl]).astype(o_ref.dtype)
    elif epilogue == "logsigmoid":
        z = y + rest[0][...]
        o_ref[...] = (jnp.minimum(z, 0.0) - jnp.log1p(jnp.exp(-jnp.abs(z)))).astype(o_ref.dtype)
    elif epilogue == "residual":
        x_ref, g_ref = rest[0], rest[1]
        o_ref[...] = (x_ref[...] + g_ref[...] * y).astype(o_ref.dtype)
    else:
        raise ValueError(epilogue)


def _proj_kernel(a_ref, w_ref, *rest, epilogue, w_transposed):
    wb_ref = rest[-1]
    o_ref = rest[-2]

    @pl.when(pl.program_id(1) == 0)
    def _():
        _cast_weight_tile(w_ref, wb_ref)

    if w_transposed:
        y = lax.dot_general(a_ref[...], wb_ref[...], (((1,), (1,)), ((), ())),
                            preferred_element_type=F32)
    else:
        y = jnp.dot(a_ref[...], wb_ref[...], preferred_element_type=F32)
    _epilogue(y, rest[:-2], o_ref, epilogue)


def _proj(a, w, *, n, col0, out_dtype, epilogue, tm, tn, w_transposed=False, row_vecs=(), tiles=(),
          name):
    m, k = a.shape
    assert n % tn == 0 and m % tm == 0
    if w_transposed:
        assert col0 % SUBLANES == 0
        w_spec = pl.BlockSpec((pl.Element(tn), pl.Element(k)),
                              lambda j, i: (pl.multiple_of(col0 + j * tn, SUBLANES), 0))
        wb_shape = (tn, k)
    else:
        assert col0 % tn == 0
        w_spec = pl.BlockSpec((k, tn), lambda j, i: (0, col0 // tn + j))
        wb_shape = (k, tn)
    in_specs = [pl.BlockSpec((tm, k), lambda j, i: (i, 0)), w_spec]
    args = [a, w]
    for t in tiles:
        in_specs.append(pl.BlockSpec((tm, tn), lambda j, i: (i, j)))
        args.append(t)
    for v in row_vecs:
        in_specs.append(pl.BlockSpec((1, tn), lambda j, i: (0, j)))
        args.append(v.reshape(1, n))
    return pl.pallas_call(
        functools.partial(_proj_kernel, epilogue=epilogue, w_transposed=w_transposed),
        grid=(n // tn, m // tm),
        in_specs=in_specs,
        out_specs=pl.BlockSpec((tm, tn), lambda j, i: (i, j)),
        out_shape=jax.ShapeDtypeStruct((m, n), out_dtype),
        scratch_shapes=[pltpu.VMEM(wb_shape, BF16)],
        compiler_params=_params(("parallel", "arbitrary"), 56),
        name=name,
    )(*args)


def _cumsum_kernel(x_ref, ft_ref, *, chunk):
    s = x_ref.shape[0]
    row = lax.broadcasted_iota(jnp.int32, (chunk, LANES), 0)

    def body(c, carry):
        s0 = pl.multiple_of(c * chunk, chunk)
        x = x_ref[pl.ds(s0, chunk), :]
        d = 1
        while d < chunk:
            x = x + jnp.where(row >= d, pltpu.roll(x, d, 0), 0.0)
            d *= 2
        x = x + carry
        ft_ref[:, pl.ds(s0, chunk)] = x.T
        return x[chunk - 1:chunk, :]

    lax.fori_loop(0, s // chunk, body, jnp.zeros((1, LANES), F32))


def _cumsum_t(logf, *, chunk=512):
    s = logf.shape[0]
    return pl.pallas_call(
        functools.partial(_cumsum_kernel, chunk=chunk),
        out_shape=jax.ShapeDtypeStruct((LANES, s), F32),
        compiler_params=_params(None, 40),
        name="cumsum",
    )(logf)


N_SPLIT = 3


def _decay_features(ft_ref, qf_ref, kf_ref, *, chunk=512):
    s = ft_ref.shape[1]
    row = lax.broadcasted_iota(jnp.int32, (LANES, chunk), 0)

    def body(c, carry):
        c0 = pl.multiple_of(c * chunk, chunk)
        rem = ft_ref[:, pl.ds(c0, chunk)] * LOG2E
        qf = jnp.where((row >= N_SPLIT) & (row < 2 * N_SPLIT), 1.0, 0.0)
        kf = jnp.where(row < N_SPLIT, 1.0, 0.0)
        for p in range(N_SPLIT):
            piece = rem.astype(BF16).astype(F32)
            rem = rem - piece
            qf = jnp.where(row == p, piece, qf)
            kf = jnp.where(row == N_SPLIT + p, -piece, kf)
        qf_ref[pl.ds(c0, chunk), :] = qf.T.astype(BF16)
        kf_ref[pl.ds(c0, chunk), :] = kf.T.astype(BF16)
        return carry

    lax.fori_loop(0, s // chunk, body, 0)


def _attn_kernel(q_ref, k_ref, v_ref, ft_ref, z_ref, o_ref, qf_ref, kf_ref, va_ref, qa_ref, sa_ref,
                 sb_ref, m_ref, acc_ref, *, tq, tk):
    i = pl.program_id(1)
    reps = tk // LANES

    @pl.when(i == 0)
    def _():
        _decay_features(ft_ref, qf_ref, kf_ref)
        lane = lax.broadcasted_iota(jnp.int32, (tk, LANES), 1)
        ones_col = jnp.where(lane == 0, 1.0, 0.0).astype(BF16)

        def body(c, carry):
            c0 = pl.multiple_of(c * tk, tk)
            va_ref[pl.ds(c0, tk), :HEAD_DIM] = v_ref[pl.ds(c0, tk), :]
            va_ref[pl.ds(c0, tk), HEAD_DIM:] = ones_col
            return carry

        lax.fori_loop(0, v_ref.shape[0] // tk, body, 0)

    q0 = pl.multiple_of(i * tq, tq)
    qa_ref[:, :HEAD_DIM] = q_ref[...]
    qa_ref[:, HEAD_DIM:] = qf_ref[pl.ds(q0, tq), :]

    m_ref[...] = jnp.full(m_ref.shape, MASK_VALUE, F32)
    acc_ref[...] = jnp.zeros(acc_ref.shape, F32)

    def scores(j, dst_ref, r0=0):
        k0 = pl.multiple_of(j * tk, tk)
        ka = jnp.concatenate([k_ref[pl.ds(k0, tk), :], kf_ref[pl.ds(k0, tk), :]], axis=1)
        dst_ref[r0:, :] = lax.dot_general(qa_ref[r0:, :], ka, (((1,), (1,)), ((), ())),
                                          preferred_element_type=F32)

    def consume(src_ref, j, r0=0, causal=False):
        k0 = pl.multiple_of(j * tk, tk)
        nr = tq - r0
        s2 = src_ref[r0:, :]
        if causal:
            rows = lax.broadcasted_iota(jnp.int32, (nr, tk), 0)
            cols = lax.broadcasted_iota(jnp.int32, (nr, tk), 1)
            s2 = jnp.where(rows >= cols, s2, MASK_VALUE)
        m_prev = m_ref[r0:, :]
        m_new = jnp.maximum(m_prev, jnp.max(s2, axis=1, keepdims=True))
        alpha = jnp.exp2(m_prev - m_new)
        p = jnp.exp2((s2 - jnp.concatenate([m_new] * reps, axis=1)).astype(BF16))
        pv = jnp.dot(p, va_ref[pl.ds(k0, tk), :], preferred_element_type=F32)
        acc_ref[r0:, :] = jnp.concatenate([alpha, alpha], axis=1) * acc_ref[r0:, :] + pv
        m_ref[r0:, :] = m_new

    nk = tq // tk
    scores(0, sa_ref)

    def pair(jj):
        j = 2 * jj
        scores(j + 1, sb_ref)
        consume(sa_ref, j)
        scores(j + 2, sa_ref)
        consume(sb_ref, j + 1)

    def quad(qq, carry):
        pair(2 * qq)
        pair(2 * qq + 1)
        return carry

    n_pairs = i * (nk // 2)
    lax.fori_loop(0, n_pairs // 2, quad, 0)

    @pl.when(n_pairs % 2 == 1)
    def _():
        pair(n_pairs - 1)

    bufs = (sa_ref, sb_ref)
    for d in range(nk):
        if d + 1 < nk:
            scores(nk * i + d + 1, bufs[(d + 1) % 2], r0=(d + 1) * tk)
        consume(bufs[d % 2], nk * i + d, r0=d * tk, causal=True)

    acc = acc_ref[...]
    att = acc[:, :HEAD_DIM] / acc[:, HEAD_DIM:HEAD_DIM + 1]
    o_ref[...] = (att * z_ref[...].astype(F32)).astype(o_ref.dtype)


def _attention(qk, v, z, ft, *, heads, tq=1024, tk=512):
    s = qk.shape[0]
    assert tq % (2 * tk) == 0 and s % tq == 0
    return pl.pallas_call(
        functools.partial(_attn_kernel, tq=tq, tk=tk),
        grid=(heads, s // tq),
        in_specs=[pl.BlockSpec((tq, HEAD_DIM), lambda h, i: (i, h)),
                  pl.BlockSpec((s, HEAD_DIM), lambda h, i: (0, heads + h)),
                  pl.BlockSpec((s, HEAD_DIM), lambda h, i: (0, h)),
                  pl.BlockSpec((None, 1, s), lambda h, i: (h, 0, 0)),
                  pl.BlockSpec((tq, HEAD_DIM), lambda h, i: (i, h))],
        out_specs=pl.BlockSpec((tq, HEAD_DIM), lambda h, i: (i, h)),
        out_shape=jax.ShapeDtypeStruct((s, heads * HEAD_DIM), BF16),
        scratch_shapes=[pltpu.VMEM((s, LANES), BF16),
                        pltpu.VMEM((s, LANES), BF16),
                        pltpu.VMEM((s, HEAD_DIM + LANES), BF16),
                        pltpu.VMEM((tq, HEAD_DIM + LANES), BF16),
                        pltpu.VMEM((tq, tk), F32),
                        pltpu.VMEM((tq, tk), F32),
                        pltpu.VMEM((tq, LANES), F32),
                        pltpu.VMEM((tq, HEAD_DIM + LANES), F32)],
        compiler_params=_params(("parallel", "arbitrary"), 48),
        name="attn",
    )(qk, qk, v, ft, z)


def _rnn_kernel(x_ref, z_ref, cw_ref, cb_ref, wax_ref, ba_ref, bx_ref, lam_ref, o_ref,
                xprev_ref, hc_ref, a_ref, b_ref, *, tt, wt):
    t = pl.program_id(1)
    groups = tt // SUBLANES

    @pl.when(t == 0)
    def _():
        xprev_ref[...] = jnp.zeros(xprev_ref.shape, F32)
        hc_ref[...] = jnp.zeros(hc_ref.shape, F32)

    x = x_ref[...]
    xp = xprev_ref[...]
    row8 = lax.broadcasted_iota(jnp.int32, (SUBLANES, wt), 0)
    xc = cb_ref[...] + x * cw_ref[CONV_W - 1:CONV_W, :]
    for j in range(1, CONV_W):
        r = pltpu.roll(x, j, 0)
        head = jnp.where(row8 < j, pltpu.roll(xp, j, 0), r[:SUBLANES])
        xs = jnp.concatenate([head, r[SUBLANES:]], axis=0)
        xc = xc + xs * cw_ref[CONV_W - 1 - j:CONV_W - j, :]
    xprev_ref[...] = x[tt - SUBLANES:, :]

    xcb = xc.astype(BF16)
    ga, gx = [], []
    for n in range(wt // LANES):
        g = jnp.dot(xcb[:, n * LANES:(n + 1) * LANES], wax_ref[n], preferred_element_type=F32)
        ga.append(g[:, :LANES])
        gx.append(g[:, LANES:])
    r_gate = _sigmoid(jnp.concatenate(ga, axis=1) + ba_ref[...])
    i_gate = _sigmoid(jnp.concatenate(gx, axis=1) + bx_ref[...])
    log_a = (-RG_C) * r_gate * _softplus(-lam_ref[...])
    a = jnp.exp(log_a)
    mult = jnp.sqrt(-jnp.tanh(log_a) * (a * a + 1.0))
    b = mult * (i_gate * xc)

    a3 = a.reshape(groups, SUBLANES, wt)
    b3 = b.reshape(groups, SUBLANES, wt)
    sub = lax.broadcasted_iota(jnp.int32, (groups, SUBLANES, wt), 1)
    d = 1
    while d < SUBLANES:
        keep = sub >= d
        b3 = a3 * jnp.where(keep, pltpu.roll(b3, d, 1), 0.0) + b3
        a3 = a3 * jnp.where(keep, pltpu.roll(a3, d, 1), 1.0)
        d *= 2
    a_ref[...] = a3.reshape(tt, wt)
    b_ref[...] = b3.reshape(tt, wt)

    def body(g, hprev):
        g0 = pl.multiple_of(g * SUBLANES, SUBLANES)
        hg = b_ref[pl.ds(g0, SUBLANES), :] + a_ref[pl.ds(g0, SUBLANES), :] * hprev
        b_ref[pl.ds(g0, SUBLANES), :] = hg
        return jnp.broadcast_to(hg[SUBLANES - 1:SUBLANES, :], (SUBLANES, wt))

    hlast = lax.fori_loop(0, groups, body, hc_ref[...], unroll=8)
    hc_ref[...] = hlast
    o_ref[...] = (b_ref[...] * z_ref[...].astype(F32)).astype(o_ref.dtype)


def _rnn(x_r, z_r, conv_w, conv_b, w_ax, b_a, b_x, lam, *, tt=1024, wt=512):
    s, w = x_r.shape
    nb = wt // LANES
    tile = pl.BlockSpec((tt, wt), lambda n, t: (t, n))
    vec = pl.BlockSpec((1, wt), lambda n, t: (0, n))
    return pl.pallas_call(
        functools.partial(_rnn_kernel, tt=tt, wt=wt),
        grid=(w // wt, s // tt),
        in_specs=[tile, tile,
                  pl.BlockSpec((CONV_W, wt), lambda n, t: (0, n)), vec,
                  pl.BlockSpec((nb, LANES, 2 * LANES), lambda n, t: (n, 0, 0)),
                  vec, vec, vec],
        out_specs=tile,
        out_shape=jax.ShapeDtypeStruct((s, w), BF16),
        scratch_shapes=[pltpu.VMEM((SUBLANES, wt), F32),
                        pltpu.VMEM((SUBLANES, wt), F32),
                        pltpu.VMEM((tt, wt), F32),
                        pltpu.VMEM((tt, wt), F32)],
        compiler_params=_params(("parallel", "arbitrary"), 40),
        name="rnn",
    )(x_r, z_r, conv_w, conv_b.reshape(1, w), w_ax, b_a.reshape(1, w), b_x.reshape(1, w),
      lam.reshape(1, w))


def _merge_kernel(ya_ref, yr_ref, ga_ref, gr_ref, wba_ref, wbr_ref, o_ref, wa_ref, wr_ref):
    @pl.when(pl.program_id(1) == 0)
    def _():
        _cast_weight_tile(wba_ref, wa_ref)
        _cast_weight_tile(wbr_ref, wr_ref)

    y_a = jnp.dot(ya_ref[...], wa_ref[...], preferred_element_type=F32)
    acc = ga_ref[...].astype(F32) * y_a
    y_r = jnp.dot(yr_ref[...], wr_ref[...], preferred_element_type=F32)
    o_ref[...] = (acc + gr_ref[...].astype(F32) * y_r).astype(o_ref.dtype)


def _merge(ya, yr, g, w_ba, w_br, *, tm=1024, tn=512):
    s, wd = ya.shape
    d = w_ba.shape[1]
    nj = d // tn
    return pl.pallas_call(
        _merge_kernel,
        grid=(d // tn, s // tm),
        in_specs=[pl.BlockSpec((tm, wd), lambda j, i: (i, 0)),
                  pl.BlockSpec((tm, wd), lambda j, i: (i, 0)),
                  pl.BlockSpec((tm, tn), lambda j, i: (i, j)),
                  pl.BlockSpec((tm, tn), lambda j, i: (i, nj + j)),
                  pl.BlockSpec((wd, tn), lambda j, i: (0, j)),
                  pl.BlockSpec((wd, tn), lambda j, i: (0, j))],
        out_specs=pl.BlockSpec((tm, tn), lambda j, i: (i, j)),
        out_shape=jax.ShapeDtypeStruct((s, d), BF16),
        scratch_shapes=[pltpu.VMEM((wd, tn), BF16), pltpu.VMEM((wd, tn), BF16)],
        compiler_params=_params(("parallel", "arbitrary"), 56),
        name="merge",
    )(ya, yr, g, g, w_ba, w_br)


def kernel(x, c, w_ada, b_ada, norm_w, w_in, b_f, q_norm_w, k_norm_w, conv_w, conv_b, w_rg_a,
           b_rg_a, w_rg_x, b_rg_x, lru_lambda, w_br_attn, w_br_rnn, w_gate, b_gate, w_out):
    bsz, s, d = x.shape
    assert bsz == 1
    aw = w_br_attn.shape[0]
    rw = w_br_rnn.shape[0]
    heads = aw // HEAD_DIM
    x2 = x.reshape(s, d)

    mod = _ada(c, w_ada, b_ada)
    shift, scale, gate_res = mod[:, :d], mod[:, d:2 * d], mod[:, 2 * d:]
    h = _prenorm(x2, norm_w, scale, shift)

    o_f = 4 * aw
    o_xr = o_f + heads
    qk_gain = jnp.concatenate([jnp.tile(q_norm_w * (LOG2E / math.sqrt(HEAD_DIM)), heads),
                               jnp.tile(k_norm_w, heads)])
    proj = functools.partial(_proj, h, w_in.T, w_transposed=True, tm=512, tn=1024)
    qk = proj(n=2 * aw, col0=0, out_dtype=BF16, epilogue="headnorm", row_vecs=(qk_gain,),
              name="proj_qk")
    v = proj(n=aw, col0=2 * aw, out_dtype=BF16, epilogue="plain", name="proj_v")
    z_a = proj(n=aw, col0=3 * aw, out_dtype=BF16, epilogue="silu", name="proj_za")
    logf = _proj(h, w_in.T, w_transposed=True, n=LANES, col0=o_f, out_dtype=F32,
                 epilogue="logsigmoid", tm=1024, tn=LANES,
                 row_vecs=(jnp.pad(b_f, (0, LANES - heads)),), name="proj_f")
    x_r = proj(n=rw, col0=o_xr, out_dtype=F32, epilogue="plain", name="proj_xr")
    z_r = proj(n=rw, col0=o_xr + rw, out_dtype=BF16, epilogue="silu", name="proj_zr")

    ft = _cumsum_t(logf)[:heads].reshape(heads, 1, s)
    ya = _attention(qk, v, z_a, ft, heads=heads)

    w_ax = jnp.concatenate([w_rg_a, w_rg_x], axis=2).astype(BF16)
    yr = _rnn(x_r, z_r, conv_w, conv_b, w_ax, b_rg_a, b_rg_x, lru_lambda)

    g = _proj(h, w_gate, n=2 * d, col0=0, out_dtype=BF16, epilogue="sigmoid_bias", tm=512, tn=1024,
              row_vecs=(b_gate,), name="gate")
    merged = _merge(ya, yr, g, w_br_attn, w_br_rnn)
    y = _proj(merged, w_out, n=d, col0=0, out_dtype=x.dtype, epilogue="residual", tm=1024, tn=512,
              tiles=(x2,), row_vecs=(gate_res.reshape(-1),), name="out")
    return y.reshape(bsz, s, d)
```

```python
import functools
import math

import jax
import jax.numpy as jnp
from jax import lax
from jax.experimental import pallas as pl
from jax.experimental.pallas import tpu as pltpu

F32 = jnp.float32
BF16 = jnp.bfloat16

LANES = 128
SUBLANES = 8
HEAD_DIM = 128
CONV_W = 4
RG_C = 8.0
EPS = 1e-6
MASK_VALUE = -1e30
LOG2E = 1.4426950408889634
MIB = 1024 * 1024


def _params(semantics, vmem_mib, flags=None):
    return pltpu.CompilerParams(dimension_semantics=semantics,
                                vmem_limit_bytes=vmem_mib * MIB, flags=flags)


def _sigmoid(x):
    return 1.0 / (1.0 + jnp.exp(-x))


def _silu(x):
    return x * _sigmoid(x)


def _softplus(x):
    return jnp.maximum(x, 0.0) + jnp.log1p(jnp.exp(-jnp.abs(x)))


def _ada_kernel(c_ref, w_ref, b_ref, o_ref, *, rows):
    d = w_ref.shape[0]

    def body(r, acc):
        r0 = pl.multiple_of(r * rows, rows)
        c = c_ref[pl.ds(r0, rows), :]
        return acc + jnp.sum(w_ref[pl.ds(r0, rows), :] * _silu(c), axis=0, keepdims=True)

    acc = lax.fori_loop(0, d // rows, body, jnp.zeros(o_ref.shape, F32))
    o_ref[...] = acc + b_ref[...]


def _ada(c, w_ada, b_ada, *, tn=512, rows=512):
    d, n = w_ada.shape
    return pl.pallas_call(
        functools.partial(_ada_kernel, rows=rows),
        grid=(n // tn,),
        in_specs=[pl.BlockSpec((d, 1), lambda j: (0, 0)),
                  pl.BlockSpec((d, tn), lambda j: (0, j)),
                  pl.BlockSpec((1, tn), lambda j: (0, j))],
        out_specs=pl.BlockSpec((1, tn), lambda j: (0, j)),
        out_shape=jax.ShapeDtypeStruct((1, n), F32),
        compiler_params=_params(("parallel",), 40),
        name="ada",
    )(c.reshape(d, 1), w_ada, b_ada.reshape(1, n))


def _prenorm_kernel(x_ref, nw_ref, scale_ref, shift_ref, o_ref):
    x = x_ref[...]
    ms = jnp.mean(x * x, axis=-1, keepdims=True)
    h = x * lax.rsqrt(ms + EPS) * nw_ref[...]
    o_ref[...] = (h * (1.0 + scale_ref[...]) + shift_ref[...]).astype(o_ref.dtype)


def _prenorm(x, norm_w, scale, shift, *, tm=512):
    s, d = x.shape
    vec = pl.BlockSpec((1, d), lambda i: (0, 0))
    return pl.pallas_call(
        _prenorm_kernel,
        grid=(s // tm,),
        in_specs=[pl.BlockSpec((tm, d), lambda i: (i, 0)), vec, vec, vec],
        out_specs=pl.BlockSpec((tm, d), lambda i: (i, 0)),
        out_shape=jax.ShapeDtypeStruct((s, d), BF16),
        compiler_params=_params(("parallel",), 40),
        name="prenorm",
    )(x, norm_w.reshape(1, d), scale, shift)


def _cast_weight_tile(w_ref, wb_ref, *, rows=128):
    n_rows = w_ref.shape[0]

    def body(r, carry):
        r0 = pl.multiple_of(r * rows, rows)
        wb_ref[pl.ds(r0, rows), :] = w_ref[pl.ds(r0, rows), :].astype(BF16)
        return carry

    lax.fori_loop(0, n_rows // rows, body, 0)


def _epilogue(y, rest, o_ref, epilogue):
    if epilogue == "plain":
        o_ref[...] = y.astype(o_ref.dtype)
    elif epilogue == "silu":
        o_ref[...] = _silu(y).astype(o_ref.dtype)
    elif epilogue == "sigmoid_bias":
        o_ref[...] = _sigmoid(y + rest[0][...]).astype(o_ref.dtype)
    elif epilogue == "headnorm":
        gain_ref = rest[0]
        for hh in range(y.shape[1] // HEAD_DIM):
            sl = slice(hh * HEAD_DIM, (hh + 1) * HEAD_DIM)
            yh = y[:, sl]
            ms = jnp.mean(yh * yh, axis=-1, keepdims=True)
            o_ref[:, sl] = (yh * lax.rsqrt(ms + EPS) * gain_ref[:, sl]).astype(o_ref.dtype)
    elif epilogue == "logsigmoid":
        z = y + rest[0][...]
        o_ref[...] = (jnp.minimum(z, 0.0) - jnp.log1p(jnp.exp(-jnp.abs(z)))).astype(o_ref.dtype)
    elif epilogue == "residual":
        x_ref, g_ref = rest[0], rest[1]
        o_ref[...] = (x_ref[...] + g_ref[...] * y).astype(o_ref.dtype)
    else:
        raise ValueError(epilogue)


def _proj_kernel(a_ref, w_ref, *rest, epilogue, w_transposed, out_transposed):
    wb_ref = rest[-1]
    o_ref = rest[-2]

    @pl.when(pl.program_id(1) == 0)
    def _():
        _cast_weight_tile(w_ref, wb_ref)

    nt_dims = (((1,), (1,)), ((), ()))
    if out_transposed:
        y = lax.dot_general(wb_ref[...], a_ref[...], nt_dims, preferred_element_type=F32)
    elif w_transposed:
        y = lax.dot_general(a_ref[...], wb_ref[...], nt_dims, preferred_element_type=F32)
    else:
        y = jnp.dot(a_ref[...], wb_ref[...], preferred_element_type=F32)
    _epilogue(y, rest[:-2], o_ref, epilogue)


def _proj(a, w, *, n, col0, out_dtype, epilogue, tm, tn, w_transposed=False, out_transposed=False,
          row_vecs=(), tiles=(), name):
    m, k = a.shape
    assert not out_transposed or (w_transposed and not row_vecs and not tiles)
    assert n % tn == 0 and m % tm == 0
    if w_transposed:
        assert col0 % SUBLANES == 0
        w_spec = pl.BlockSpec((pl.Element(tn), pl.Element(k)),
                              lambda j, i: (pl.multiple_of(col0 + j * tn, SUBLANES), 0))
        wb_shape = (tn, k)
    else:
        assert col0 % tn == 0
        w_spec = pl.BlockSpec((k, tn), lambda j, i: (0, col0 // tn + j))
        wb_shape = (k, tn)
    in_specs = [pl.BlockSpec((tm, k), lambda j, i: (i, 0)), w_spec]
    args = [a, w]
    for t in tiles:
        in_specs.append(pl.BlockSpec((tm, tn), lambda j, i: (i, j)))
        args.append(t)
    for v in row_vecs:
        in_specs.append(pl.BlockSpec((1, tn), lambda j, i: (0, j)))
        args.append(v.reshape(1, n))
    if out_transposed:
        out_spec = pl.BlockSpec((tn, tm), lambda j, i: (j, i))
        out_shape = jax.ShapeDtypeStruct((n, m), out_dtype)
    else:
        out_spec = pl.BlockSpec((tm, tn), lambda j, i: (i, j))
        out_shape = jax.ShapeDtypeStruct((m, n), out_dtype)
    return pl.pallas_call(
        functools.partial(_proj_kernel, epilogue=epilogue, w_transposed=w_transposed,
                          out_transposed=out_transposed),
        grid=(n // tn, m // tm),
        in_specs=in_specs,
        out_specs=out_spec,
        out_shape=out_shape,
        scratch_shapes=[pltpu.VMEM(wb_shape, BF16)],
        compiler_params=_params(("parallel", "arbitrary"), 56),
        name=name,
    )(*args)


def _cumsum_kernel(x_ref, ft_ref, *, chunk):
    s = x_ref.shape[0]
    row = lax.broadcasted_iota(jnp.int32, (chunk, LANES), 0)

    def body(c, carry):
        s0 = pl.multiple_of(c * chunk, chunk)
        x = x_ref[pl.ds(s0, chunk), :]
        d = 1
        while d < chunk:
            x = x + jnp.where(row >= d, pltpu.roll(x, d, 0), 0.0)
            d *= 2
        x = x + carry
        ft_ref[:, pl.ds(s0, chunk)] = x.T
        return x[chunk - 1:chunk, :]

    lax.fori_loop(0, s // chunk, body, jnp.zeros((1, LANES), F32))


def _cumsum_t(logf, *, chunk=512):
    s = logf.shape[0]
    return pl.pallas_call(
        functools.partial(_cumsum_kernel, chunk=chunk),
        out_shape=jax.ShapeDtypeStruct((LANES, s), F32),
        compiler_params=_params(None, 40),
        name="cumsum",
    )(logf)


N_SPLIT = 3


def _decay_features(ft_ref, qf_ref, kf_ref, *, chunk=512):
    s = ft_ref.shape[1]
    row = lax.broadcasted_iota(jnp.int32, (LANES, chunk), 0)

    def body(c, carry):
        c0 = pl.multiple_of(c * chunk, chunk)
        rem = ft_ref[:, pl.ds(c0, chunk)] * LOG2E
        qf = jnp.where((row >= N_SPLIT) & (row < 2 * N_SPLIT), 1.0, 0.0)
        kf = jnp.where(row < N_SPLIT, 1.0, 0.0)
        for p in range(N_SPLIT):
            piece = rem.astype(BF16).astype(F32)
            rem = rem - piece
            qf = jnp.where(row == p, piece, qf)
            kf = jnp.where(row == N_SPLIT + p, -piece, kf)
        qf_ref[pl.ds(c0, chunk), :] = qf.T.astype(BF16)
        kf_ref[pl.ds(c0, chunk), :] = kf.T.astype(BF16)
        return carry

    lax.fori_loop(0, s // chunk, body, 0)


VA_ROWS = HEAD_DIM + 16
QUERY_CHUNK = 256


def _attn_kernel(q_ref, k_ref, vt_ref, ft_ref, z_ref, o_ref, qf_ref, kf_ref, vat_ref, qa_ref, sa_ref,
                 sb_ref, m_ref, acc_ref, *, tq, tk):
    i = pl.program_id(1)
    nt_dims = (((1,), (1,)), ((), ()))

    @pl.when(i == 0)
    def _():
        _decay_features(ft_ref, qf_ref, kf_ref)
        vat_ref[:HEAD_DIM, :] = vt_ref[...]
        row = lax.broadcasted_iota(jnp.int32, (VA_ROWS - HEAD_DIM, vat_ref.shape[1]), 0)
        vat_ref[HEAD_DIM:, :] = jnp.where(row == 0, 1.0, 0.0).astype(BF16)

    q0 = pl.multiple_of(i * tq, tq)
    qa_ref[:, :HEAD_DIM] = q_ref[...]
    qa_ref[:, HEAD_DIM:] = qf_ref[pl.ds(q0, tq), :]

    m_ref[...] = jnp.full(m_ref.shape, MASK_VALUE, F32)
    acc_ref[...] = jnp.zeros(acc_ref.shape, F32)

    def scores(j, dst_ref, c0=0):
        k0 = pl.multiple_of(j * tk, tk)
        ka = jnp.concatenate([k_ref[pl.ds(k0, tk), :], kf_ref[pl.ds(k0, tk), :]], axis=1)
        dst_ref[:, c0:] = lax.dot_general(ka, qa_ref[c0:, :], nt_dims,
                                          preferred_element_type=F32)

    def consume(src_ref, j, c0=0, causal=False):
        k0 = pl.multiple_of(j * tk, tk)
        row0 = pl.multiple_of(jnp.minimum(i, 0) * SUBLANES, SUBLANES)
        for q1 in range(c0, tq, QUERY_CHUNK):
            qs = slice(q1, q1 + QUERY_CHUNK)
            s2 = src_ref[pl.ds(row0, tk), qs]
            if causal and q1 - c0 < tk:
                keys = lax.broadcasted_iota(jnp.int32, (tk, QUERY_CHUNK), 0)
                qrys = lax.broadcasted_iota(jnp.int32, (tk, QUERY_CHUNK), 1) + (q1 - c0)
                s2 = jnp.where(qrys >= keys, s2, MASK_VALUE)
            m_prev = m_ref[:, qs]
            m_new = jnp.maximum(m_prev, jnp.max(s2, axis=0, keepdims=True))
            alpha = jnp.exp2(m_prev - m_new)
            p = jnp.exp2((s2 - m_new).astype(BF16))
            pv = jnp.dot(vat_ref[:, pl.ds(k0, tk)], p, preferred_element_type=F32)
            acc_ref[:, qs] = alpha * acc_ref[:, qs] + pv
            m_ref[:, qs] = m_new

    nk = tq // tk
    scores(0, sa_ref)

    def pair(jj):
        j = 2 * jj
        scores(j + 1, sb_ref)
        consume(sa_ref, j)
        scores(j + 2, sa_ref)
        consume(sb_ref, j + 1)

    def quad(qq, carry):
        pair(2 * qq)
        pair(2 * qq + 1)
        return carry

    n_pairs = i * (nk // 2)
    lax.fori_loop(0, n_pairs // 2, quad, 0)

    @pl.when(n_pairs % 2 == 1)
    def _():
        pair(n_pairs - 1)

    bufs = (sa_ref, sb_ref)
    for d in range(nk):
        if d + 1 < nk:
            scores(nk * i + d + 1, bufs[(d + 1) % 2], c0=(d + 1) * tk)
        consume(bufs[d % 2], nk * i + d, c0=d * tk, causal=True)

    acc = acc_ref[...]
    att = (acc[:HEAD_DIM, :] / acc[HEAD_DIM:HEAD_DIM + 1, :]).T
    o_ref[...] = (att * z_ref[...].astype(F32)).astype(o_ref.dtype)


def _attention(qk, vt, z, ft, *, heads, tq=1024, tk=512):
    s = qk.shape[0]
    assert tq % (2 * tk) == 0 and s % tq == 0
    return pl.pallas_call(
        functools.partial(_attn_kernel, tq=tq, tk=tk),
        grid=(heads, s // tq),
        in_specs=[pl.BlockSpec((tq, HEAD_DIM), lambda h, i: (i, h)),
                  pl.BlockSpec((s, HEAD_DIM), lambda h, i: (0, heads + h)),
                  pl.BlockSpec((HEAD_DIM, s), lambda h, i: (h, 0)),
                  pl.BlockSpec((None, 1, s), lambda h, i: (h, 0, 0)),
                  pl.BlockSpec((tq, HEAD_DIM), lambda h, i: (i, h))],
        out_specs=pl.BlockSpec((tq, HEAD_DIM), lambda h, i: (i, h)),
        out_shape=jax.ShapeDtypeStruct((s, heads * HEAD_DIM), BF16),
        scratch_shapes=[pltpu.VMEM((s, LANES), BF16),
                        pltpu.VMEM((s, LANES), BF16),
                        pltpu.VMEM((VA_ROWS, s), BF16),
                        pltpu.VMEM((tq, HEAD_DIM + LANES), BF16),
                        pltpu.VMEM((tk, tq), F32),
                        pltpu.VMEM((tk, tq), F32),
                        pltpu.VMEM((1, tq), F32),
                        pltpu.VMEM((VA_ROWS, tq), F32)],
        compiler_params=_params(("parallel", "arbitrary"), 48),
        name="attn",
    )(qk, qk, vt, ft, z)


def _rnn_kernel(x_ref, z_ref, cw_ref, cb_ref, wax_ref, ba_ref, bx_ref, lam_ref, o_ref,
                xpad_ref, hc_ref, a_ref, b_ref, *, tt, wt):
    t = pl.program_id(1)
    groups = tt // SUBLANES

    @pl.when(t == 0)
    def _():
        xpad_ref[:SUBLANES, :] = jnp.zeros((SUBLANES, wt), F32)
        hc_ref[...] = jnp.zeros(hc_ref.shape, F32)

    x = x_ref[...]
    xpad_ref[SUBLANES:, :] = x
    xc = cb_ref[...] + x * cw_ref[CONV_W - 1:CONV_W, :]
    for j in range(1, CONV_W):
        xc = xc + xpad_ref[pl.ds(SUBLANES - j, tt), :] * cw_ref[CONV_W - 1 - j:CONV_W - j, :]
    xpad_ref[:SUBLANES, :] = x[tt - SUBLANES:, :]

    xcb = xc.astype(BF16)
    ga, gx = [], []
    for n in range(wt // LANES):
        g = jnp.dot(xcb[:, n * LANES:(n + 1) * LANES], wax_ref[n], preferred_element_type=F32)
        ga.append(g[:, :LANES])
        gx.append(g[:, LANES:])
    r_gate = _sigmoid(jnp.concatenate(ga, axis=1) + ba_ref[...])
    i_gate = _sigmoid(jnp.concatenate(gx, axis=1) + bx_ref[...])
    log_a = (-RG_C) * r_gate * _softplus(-lam_ref[...])
    a = jnp.exp(log_a)
    mult = jnp.sqrt(-jnp.tanh(log_a) * (a * a + 1.0))
    b = mult * (i_gate * xc)

    a3 = a.reshape(groups, SUBLANES, wt)
    b3 = b.reshape(groups, SUBLANES, wt)
    sub = lax.broadcasted_iota(jnp.int32, (groups, SUBLANES, wt), 1)
    d = 1
    while d < SUBLANES:
        keep = sub >= d
        b3 = a3 * jnp.where(keep, pltpu.roll(b3, d, 1), 0.0) + b3
        a3 = a3 * jnp.where(keep, pltpu.roll(a3, d, 1), 1.0)
        d *= 2
    a_ref[...] = a3.reshape(tt, wt)
    b_ref[...] = b3.reshape(tt, wt)

    def body(g, hprev):
        g0 = pl.multiple_of(g * SUBLANES, SUBLANES)
        hg = b_ref[pl.ds(g0, SUBLANES), :] + a_ref[pl.ds(g0, SUBLANES), :] * hprev
        b_ref[pl.ds(g0, SUBLANES), :] = hg
        return jnp.broadcast_to(hg[SUBLANES - 1:SUBLANES, :], (SUBLANES, wt))

    hlast = lax.fori_loop(0, groups, body, hc_ref[...], unroll=8)
    hc_ref[...] = hlast
    o_ref[...] = (b_ref[...] * z_ref[...].astype(F32)).astype(o_ref.dtype)


def _rnn(x_r, z_r, conv_w, conv_b, w_ax, b_a, b_x, lam, *, tt=1024, wt=512):
    s, w = x_r.shape
    nb = wt // LANES
    tile = pl.BlockSpec((tt, wt), lambda n, t: (t, n))
    vec = pl.BlockSpec((1, wt), lambda n, t: (0, n))
    return pl.pallas_call(
        functools.partial(_rnn_kernel, tt=tt, wt=wt),
        grid=(w // wt, s // tt),
        in_specs=[tile, tile,
                  pl.BlockSpec((CONV_W, wt), lambda n, t: (0, n)), vec,
                  pl.BlockSpec((nb, LANES, 2 * LANES), lambda n, t: (n, 0, 0)),
                  vec, vec, vec],
        out_specs=tile,
        out_shape=jax.ShapeDtypeStruct((s, w), BF16),
        scratch_shapes=[pltpu.VMEM((SUBLANES + tt, wt), F32),
                        pltpu.VMEM((SUBLANES, wt), F32),
                        pltpu.VMEM((tt, wt), F32),
                        pltpu.VMEM((tt, wt), F32)],
        compiler_params=_params(("parallel", "arbitrary"), 40),
        name="rnn",
    )(x_r, z_r, conv_w, conv_b.reshape(1, w), w_ax, b_a.reshape(1, w), b_x.reshape(1, w),
      lam.reshape(1, w))


def _merge_kernel(ya_ref, yr_ref, ga_ref, gr_ref, wba_ref, wbr_ref, o_ref, wa_ref, wr_ref):
    @pl.when(pl.program_id(1) == 0)
    def _():
        _cast_weight_tile(wba_ref, wa_ref)
        _cast_weight_tile(wbr_ref, wr_ref)

    y_a = jnp.dot(ya_ref[...], wa_ref[...], preferred_element_type=F32)
    acc = ga_ref[...].astype(F32) * y_a
    y_r = jnp.dot(yr_ref[...], wr_ref[...], preferred_element_type=F32)
    o_ref[...] = (acc + gr_ref[...].astype(F32) * y_r).astype(o_ref.dtype)


def _merge(ya, yr, g, w_ba, w_br, *, tm=1024, tn=512):
    s, wd = ya.shape
    d = w_ba.shape[1]
    nj = d // tn
    return pl.pallas_call(
        _merge_kernel,
        grid=(d // tn, s // tm),
        in_specs=[pl.BlockSpec((tm, wd), lambda j, i: (i, 0)),
                  pl.BlockSpec((tm, wd), lambda j, i: (i, 0)),
                  pl.BlockSpec((tm, tn), lambda j, i: (i, j)),
                  pl.BlockSpec((tm, tn), lambda j, i: (i, nj + j)),
                  pl.BlockSpec((wd, tn), lambda j, i: (0, j)),
                  pl.BlockSpec((wd, tn), lambda j, i: (0, j))],
        out_specs=pl.BlockSpec((tm, tn), lambda j, i: (i, j)),
        out_shape=jax.ShapeDtypeStruct((s, d), BF16),
        scratch_shapes=[pltpu.VMEM((wd, tn), BF16), pltpu.VMEM((wd, tn), BF16)],
        compiler_params=_params(("parallel", "arbitrary"), 56),
        name="merge",
    )(ya, yr, g, g, w_ba, w_br)


def kernel(x, c, w_ada, b_ada, norm_w, w_in, b_f, q_norm_w, k_norm_w, conv_w, conv_b, w_rg_a,
           b_rg_a, w_rg_x, b_rg_x, lru_lambda, w_br_attn, w_br_rnn, w_gate, b_gate, w_out):
    bsz, s, d = x.shape
    assert bsz == 1
    aw = w_br_attn.shape[0]
    rw = w_br_rnn.shape[0]
    heads = aw // HEAD_DIM
    x2 = x.reshape(s, d)

    mod = _ada(c, w_ada, b_ada)
    shift, scale, gate_res = mod[:, :d], mod[:, d:2 * d], mod[:, 2 * d:]
    h = _prenorm(x2, norm_w, scale, shift)

    o_f = 4 * aw
    o_xr = o_f + heads
    qk_gain = jnp.concatenate([jnp.tile(q_norm_w * (LOG2E / math.sqrt(HEAD_DIM)), heads),
                               jnp.tile(k_norm_w, heads)])
    proj = functools.partial(_proj, h, w_in.T, w_transposed=True, tm=512, tn=1024)
    qk = proj(n=2 * aw, col0=0, out_dtype=BF16, epilogue="headnorm", row_vecs=(qk_gain,),
              name="proj_qk")
    vt = proj(n=aw, col0=2 * aw, out_dtype=BF16, epilogue="plain", out_transposed=True,
              name="proj_v")
    z_a = proj(n=aw, col0=3 * aw, out_dtype=BF16, epilogue="silu", name="proj_za")
    logf = _proj(h, w_in.T, w_transposed=True, n=LANES, col0=o_f, out_dtype=F32,
                 epilogue="logsigmoid", tm=1024, tn=LANES,
                 row_vecs=(jnp.pad(b_f, (0, LANES - heads)),), name="proj_f")
    x_r = proj(n=rw, col0=o_xr, out_dtype=F32, epilogue="plain", name="proj_xr")
    z_r = proj(n=rw, col0=o_xr + rw, out_dtype=BF16, epilogue="silu", name="proj_zr")

    ft = _cumsum_t(logf)[:heads].reshape(heads, 1, s)
    ya = _attention(qk, vt, z_a, ft, heads=heads)

    w_ax = jnp.concatenate([w_rg_a, w_rg_x], axis=2).astype(BF16)
    yr = _rnn(x_r, z_r, conv_w, conv_b, w_ax, b_rg_a, b_rg_x, lru_lambda)

    g = _proj(h, w_gate, n=2 * d, col0=0, out_dtype=BF16, epilogue="sigmoid_bias", tm=512, tn=1024,
              row_vecs=(b_gate,), name="gate")
    merged = _merge(ya, yr, g, w_br_attn, w_br_rnn)
    y = _proj(merged, w_out, n=d, col0=0, out_dtype=x.dtype, epilogue="residual", tm=1024, tn=512,
              tiles=(x2,), row_vecs=(gate_res.reshape(-1),), name="out")
    return y.reshape(bsz, s, d)
```

```python
import functools
import math

import jax
import jax.numpy as jnp
from jax import lax
from jax.experimental import pallas as pl
from jax.experimental.pallas import tpu as pltpu

F32 = jnp.float32
BF16 = jnp.bfloat16

LANES = 128
SUBLANES = 8
HEAD_DIM = 128
CONV_W = 4
RG_C = 8.0
EPS = 1e-6
MASK_VALUE = -1e30
LOG2E = 1.4426950408889634
MIB = 1024 * 1024


def _params(semantics, vmem_mib, flags=None):
    return pltpu.CompilerParams(dimension_semantics=semantics,
                                vmem_limit_bytes=vmem_mib * MIB, flags=flags)


def _sigmoid(x):
    return 1.0 / (1.0 + jnp.exp(-x))


def _silu(x):
    return x * _sigmoid(x)


def _softplus(x):
    return jnp.maximum(x, 0.0) + jnp.log1p(jnp.exp(-jnp.abs(x)))


def _ada_kernel(c_ref, w_ref, b_ref, o_ref, *, rows):
    d = w_ref.shape[0]

    def body(r, acc):
        r0 = pl.multiple_of(r * rows, rows)
        c = c_ref[pl.ds(r0, rows), :]
        return acc + jnp.sum(w_ref[pl.ds(r0, rows), :] * _silu(c), axis=0, keepdims=True)

    acc = lax.fori_loop(0, d // rows, body, jnp.zeros(o_ref.shape, F32))
    o_ref[...] = acc + b_ref[...]


def _ada(c, w_ada, b_ada, *, tn=512, rows=512):
    d, n = w_ada.shape
    return pl.pallas_call(
        functools.partial(_ada_kernel, rows=rows),
        grid=(n // tn,),
        in_specs=[pl.BlockSpec((d, 1), lambda j: (0, 0)),
                  pl.BlockSpec((d, tn), lambda j: (0, j)),
                  pl.BlockSpec((1, tn), lambda j: (0, j))],
        out_specs=pl.BlockSpec((1, tn), lambda j: (0, j)),
        out_shape=jax.ShapeDtypeStruct((1, n), F32),
        compiler_params=_params(("parallel",), 40),
        name="ada",
    )(c.reshape(d, 1), w_ada, b_ada.reshape(1, n))


def _prenorm_kernel(x_ref, nw_ref, scale_ref, shift_ref, o_ref):
    x = x_ref[...]
    ms = jnp.mean(x * x, axis=-1, keepdims=True)
    h = x * lax.rsqrt(ms + EPS) * nw_ref[...]
    o_ref[...] = (h * (1.0 + scale_ref[...]) + shift_ref[...]).astype(o_ref.dtype)


def _prenorm(x, norm_w, scale, shift, *, tm=512):
    s, d = x.shape
    vec = pl.BlockSpec((1, d), lambda i: (0, 0))
    return pl.pallas_call(
        _prenorm_kernel,
        grid=(s // tm,),
        in_specs=[pl.BlockSpec((tm, d), lambda i: (i, 0)), vec, vec, vec],
        out_specs=pl.BlockSpec((tm, d), lambda i: (i, 0)),
        out_shape=jax.ShapeDtypeStruct((s, d), BF16),
        compiler_params=_params(("parallel",), 40),
        name="prenorm",
    )(x, norm_w.reshape(1, d), scale, shift)


def _cast_weight_tile(w_ref, wb_ref, *, rows=128):
    n_rows = w_ref.shape[0]

    def body(r, carry):
        r0 = pl.multiple_of(r * rows, rows)
        wb_ref[pl.ds(r0, rows), :] = w_ref[pl.ds(r0, rows), :].astype(BF16)
        return carry

    lax.fori_loop(0, n_rows // rows, body, 0)


def _epilogue(y, rest, o_ref, epilogue):
    if epilogue == "plain":
        o_ref[...] = y.astype(o_ref.dtype)
    elif epilogue == "silu":
        o_ref[...] = _silu(y).astype(o_ref.dtype)
    elif epilogue == "sigmoid_bias":
        o_ref[...] = _sigmoid(y + rest[0][...]).astype(o_ref.dtype)
    elif epilogue == "headnorm":
        gain_ref = rest[0]
        for hh in range(y.shape[1] // HEAD_DIM):
            sl = slice(hh * HEAD_DIM, (hh + 1) * HEAD_DIM)
            yh = y[:, sl]
            ms = jnp.mean(yh * yh, axis=-1, keepdims=True)
            o_ref[:, sl] = (yh * lax.rsqrt(ms + EPS) * gain_ref[:, sl]).astype(o_ref.dtype)
    elif epilogue == "logsigmoid":
        z = y + rest[0][...]
        o_ref[...] = (jnp.minimum(z, 0.0) - jnp.log1p(jnp.exp(-jnp.abs(z)))).astype(o_ref.dtype)
    elif epilogue == "residual":
        x_ref, g_ref = rest[0], rest[1]
        o_ref[...] = (x_ref[...] + g_ref[...] * y).astype(o_ref.dtype)
    else:
        raise ValueError(epilogue)


def _proj_kernel(a_ref, w_ref, *rest, epilogue, w_transposed, out_transposed):
    wb_ref = rest[-1]
    o_ref = rest[-2]

    @pl.when(pl.program_id(1) == 0)
    def _():
        _cast_weight_tile(w_ref, wb_ref)

    nt_dims = (((1,), (1,)), ((), ()))
    if out_transposed:
        y = lax.dot_general(wb_ref[...], a_ref[...], nt_dims, preferred_element_type=F32)
    elif w_transposed:
        y = lax.dot_general(a_ref[...], wb_ref[...], nt_dims, preferred_element_type=F32)
    else:
        y = jnp.dot(a_ref[...], wb_ref[...], preferred_element_type=F32)
    _epilogue(y, rest[:-2], o_ref, epilogue)


def _proj(a, w, *, n, col0, out_dtype, epilogue, tm, tn, w_transposed=False, out_transposed=False,
          row_vecs=(), tiles=(), name):
    m, k = a.shape
    assert not out_transposed or (w_transposed and not row_vecs and not tiles)
    assert n % tn == 0 and m % tm == 0
    if w_transposed:
        assert col0 % SUBLANES == 0
        w_spec = pl.BlockSpec((pl.Element(tn), pl.Element(k)),
                              lambda j, i: (pl.multiple_of(col0 + j * tn, SUBLANES), 0))
        wb_shape = (tn, k)
    else:
        assert col0 % tn == 0
        w_spec = pl.BlockSpec((k, tn), lambda j, i: (0, col0 // tn + j))
        wb_shape = (k, tn)
    in_specs = [pl.BlockSpec((tm, k), lambda j, i: (i, 0)), w_spec]
    args = [a, w]
    for t in tiles:
        in_specs.append(pl.BlockSpec((tm, tn), lambda j, i: (i, j)))
        args.append(t)
    for v in row_vecs:
        in_specs.append(pl.BlockSpec((1, tn), lambda j, i: (0, j)))
        args.append(v.reshape(1, n))
    if out_transposed:
        out_spec = pl.BlockSpec((tn, tm), lambda j, i: (j, i))
        out_shape = jax.ShapeDtypeStruct((n, m), out_dtype)
    else:
        out_spec = pl.BlockSpec((tm, tn), lambda j, i: (i, j))
        out_shape = jax.ShapeDtypeStruct((m, n), out_dtype)
    return pl.pallas_call(
        functools.partial(_proj_kernel, epilogue=epilogue, w_transposed=w_transposed,
                          out_transposed=out_transposed),
        grid=(n // tn, m // tm),
        in_specs=in_specs,
        out_specs=out_spec,
        out_shape=out_shape,
        scratch_shapes=[pltpu.VMEM(wb_shape, BF16)],
        compiler_params=_params(("parallel", "arbitrary"), 56),
        name=name,
    )(*args)


W_CHUNKS = 4


def _proj_stream_kernel(a_ref, wc_ref, *rest, epilogue, w_transposed, out_transposed, ni):
    wb_ref = rest[-1]
    o_ref = rest[-2]
    s = pl.program_id(0)
    rows = wc_ref.shape[0]
    u = jnp.maximum(s - W_CHUNKS, 0)
    j = u // ni
    chunk = jnp.minimum(u % ni, W_CHUNKS - 1)

    def cast_chunk(slot, c):
        r0 = pl.multiple_of(c * rows, rows)
        wb_ref[slot, pl.ds(r0, rows), :] = wc_ref[...].astype(BF16)

    @pl.when(s < W_CHUNKS)
    def _():
        cast_chunk(0, s)

    @pl.when(s >= W_CHUNKS)
    def _():
        wb = wb_ref[j % 2]
        nt_dims = (((1,), (1,)), ((), ()))
        if out_transposed:
            y = lax.dot_general(wb, a_ref[...], nt_dims, preferred_element_type=F32)
        elif w_transposed:
            y = lax.dot_general(a_ref[...], wb, nt_dims, preferred_element_type=F32)
        else:
            y = jnp.dot(a_ref[...], wb, preferred_element_type=F32)
        _epilogue(y, rest[:-2], o_ref, epilogue)
        cast_chunk((j + 1) % 2, chunk)


def _proj_stream(a, w, *, n, col0, out_dtype, epilogue, tm, tn, w_transposed=False,
                 out_transposed=False, row_vecs=(), name):
    m, k = a.shape
    nj, ni = n // tn, m // tm
    assert n % tn == 0 and m % tm == 0 and ni >= W_CHUNKS
    assert not out_transposed or (w_transposed and not row_vecs)

    def where_to(s):
        u = jnp.maximum(s - W_CHUNKS, 0)
        j, i = u // ni, u % ni
        pro = s < W_CHUNKS
        last = j == nj - 1
        tile = jnp.where(pro, 0, jnp.minimum(j + 1, nj - 1))
        chunk = jnp.where(pro, s, jnp.where(last, W_CHUNKS - 1, jnp.minimum(i, W_CHUNKS - 1)))
        return tile, chunk, j, i

    if w_transposed:
        assert col0 % SUBLANES == 0 and tn % W_CHUNKS == 0
        cr = tn // W_CHUNKS

        def w_index(s):
            tile, chunk, _, _ = where_to(s)
            return pl.multiple_of(col0 + tile * tn + chunk * cr, SUBLANES), 0

        w_spec = pl.BlockSpec((pl.Element(cr), pl.Element(k)), w_index)
        wb_shape = (2, tn, k)
    else:
        assert col0 % tn == 0 and k % W_CHUNKS == 0
        ck = k // W_CHUNKS

        def w_index(s):
            tile, chunk, _, _ = where_to(s)
            return chunk, col0 // tn + tile

        w_spec = pl.BlockSpec((ck, tn), w_index)
        wb_shape = (2, k, tn)

    def a_index(s):
        _, _, _, i = where_to(s)
        return i, 0

    def vec_index(s):
        _, _, j, _ = where_to(s)
        return 0, j

    def out_index(s):
        _, _, j, i = where_to(s)
        return (j, i) if out_transposed else (i, j)

    in_specs = [pl.BlockSpec((tm, k), a_index), w_spec]
    args = [a, w]
    for v in row_vecs:
        in_specs.append(pl.BlockSpec((1, tn), vec_index))
        args.append(v.reshape(1, n))
    if out_transposed:
        out_spec = pl.BlockSpec((tn, tm), out_index)
        out_shape = jax.ShapeDtypeStruct((n, m), out_dtype)
    else:
        out_spec = pl.BlockSpec((tm, tn), out_index)
        out_shape = jax.ShapeDtypeStruct((m, n), out_dtype)
    return pl.pallas_call(
        functools.partial(_proj_stream_kernel, epilogue=epilogue, w_transposed=w_transposed,
                          out_transposed=out_transposed, ni=ni),
        grid=(W_CHUNKS + nj * ni,),
        in_specs=in_specs,
        out_specs=out_spec,
        out_shape=out_shape,
        scratch_shapes=[pltpu.VMEM(wb_shape, BF16)],
        compiler_params=_params(("arbitrary",), 56),
        name=name,
    )(*args)


def _cumsum_kernel(x_ref, ft_ref, *, chunk):
    s = x_ref.shape[0]
    row = lax.broadcasted_iota(jnp.int32, (chunk, LANES), 0)

    def body(c, carry):
        s0 = pl.multiple_of(c * chunk, chunk)
        x = x_ref[pl.ds(s0, chunk), :]
        d = 1
        while d < chunk:
            x = x + jnp.where(row >= d, pltpu.roll(x, d, 0), 0.0)
            d *= 2
        x = x + carry
        ft_ref[:, pl.ds(s0, chunk)] = x.T
        return x[chunk - 1:chunk, :]

    lax.fori_loop(0, s // chunk, body, jnp.zeros((1, LANES), F32))


def _cumsum_t(logf, *, chunk=512):
    s = logf.shape[0]
    return pl.pallas_call(
        functools.partial(_cumsum_kernel, chunk=chunk),
        out_shape=jax.ShapeDtypeStruct((LANES, s), F32),
        compiler_params=_params(None, 40),
        name="cumsum",
    )(logf)


N_SPLIT = 3


def _decay_features(ft_ref, qf_ref, kf_ref, *, chunk=512):
    s = ft_ref.shape[1]
    row = lax.broadcasted_iota(jnp.int32, (LANES, chunk), 0)

    def body(c, carry):
        c0 = pl.multiple_of(c * chunk, chunk)
        rem = ft_ref[:, pl.ds(c0, chunk)] * LOG2E
        qf = jnp.where((row >= N_SPLIT) & (row < 2 * N_SPLIT), 1.0, 0.0)
        kf = jnp.where(row < N_SPLIT, 1.0, 0.0)
        for p in range(N_SPLIT):
            piece = rem.astype(BF16).astype(F32)
            rem = rem - piece
            qf = jnp.where(row == p, piece, qf)
            kf = jnp.where(row == N_SPLIT + p, -piece, kf)
        qf_ref[pl.ds(c0, chunk), :] = qf.T.astype(BF16)
        kf_ref[pl.ds(c0, chunk), :] = kf.T.astype(BF16)
        return carry

    lax.fori_loop(0, s // chunk, body, 0)


VA_ROWS = HEAD_DIM + 16
QUERY_CHUNK = 256


def _attn_kernel(q_ref, k_ref, vt_ref, ft_ref, z_ref, o_ref, qf_ref, kf_ref, vat_ref, qa_ref, sa_ref,
                 sb_ref, m_ref, acc_ref, *, tq, tk):
    i = pl.program_id(1)
    nt_dims = (((1,), (1,)), ((), ()))

    @pl.when(i == 0)
    def _():
        _decay_features(ft_ref, qf_ref, kf_ref)
        vat_ref[:HEAD_DIM, :] = vt_ref[...]
        row = lax.broadcasted_iota(jnp.int32, (VA_ROWS - HEAD_DIM, vat_ref.shape[1]), 0)
        vat_ref[HEAD_DIM:, :] = jnp.where(row == 0, 1.0, 0.0).astype(BF16)

    q0 = pl.multiple_of(i * tq, tq)
    qa_ref[:, :HEAD_DIM] = q_ref[...]
    qa_ref[:, HEAD_DIM:] = qf_ref[pl.ds(q0, tq), :]

    m_ref[...] = jnp.full(m_ref.shape, MASK_VALUE, F32)
    acc_ref[...] = jnp.zeros(acc_ref.shape, F32)

    def scores(j, dst_ref, c0=0):
        k0 = pl.multiple_of(j * tk, tk)
        ka = jnp.concatenate([k_ref[pl.ds(k0, tk), :], kf_ref[pl.ds(k0, tk), :]], axis=1)
        dst_ref[:, c0:] = lax.dot_general(ka, qa_ref[c0:, :], nt_dims,
                                          preferred_element_type=F32)

    def consume(src_ref, j, c0=0, causal=False):
        k0 = pl.multiple_of(j * tk, tk)
        row0 = pl.multiple_of(jnp.minimum(i, 0) * SUBLANES, SUBLANES)
        for q1 in range(c0, tq, QUERY_CHUNK):
            qs = slice(q1, q1 + QUERY_CHUNK)
            s2 = src_ref[pl.ds(row0, tk), qs]
            if causal and q1 - c0 < tk:
                keys = lax.broadcasted_iota(jnp.int32, (tk, QUERY_CHUNK), 0)
                qrys = lax.broadcasted_iota(jnp.int32, (tk, QUERY_CHUNK), 1) + (q1 - c0)
                s2 = jnp.where(qrys >= keys, s2, MASK_VALUE)
            m_prev = m_ref[:, qs]
            m_new = jnp.maximum(m_prev, jnp.max(s2, axis=0, keepdims=True))
            alpha = jnp.exp2(m_prev - m_new)
            p = jnp.exp2((s2 - m_new).astype(BF16))
            pv = jnp.dot(vat_ref[:, pl.ds(k0, tk)], p, preferred_element_type=F32)
            acc_ref[:, qs] = alpha * acc_ref[:, qs] + pv
            m_ref[:, qs] = m_new

    nk = tq // tk
    scores(0, sa_ref)

    def pair(jj):
        j = 2 * jj
        scores(j + 1, sb_ref)
        consume(sa_ref, j)
        scores(j + 2, sa_ref)
        consume(sb_ref, j + 1)

    def quad(qq, carry):
        pair(2 * qq)
        pair(2 * qq + 1)
        return carry

    n_pairs = i * (nk // 2)
    lax.fori_loop(0, n_pairs // 2, quad, 0)

    @pl.when(n_pairs % 2 == 1)
    def _():
        pair(n_pairs - 1)

    bufs = (sa_ref, sb_ref)
    for d in range(nk):
        if d + 1 < nk:
            scores(nk * i + d + 1, bufs[(d + 1) % 2], c0=(d + 1) * tk)
        consume(bufs[d % 2], nk * i + d, c0=d * tk, causal=True)

    acc = acc_ref[...]
    att = (acc[:HEAD_DIM, :] / acc[HEAD_DIM:HEAD_DIM + 1, :]).T
    o_ref[...] = (att * z_ref[...].astype(F32)).astype(o_ref.dtype)


def _attention(qk, vt, z, ft, *, heads, tq=1024, tk=512):
    s = qk.shape[0]
    assert tq % (2 * tk) == 0 and s % tq == 0
    return pl.pallas_call(
        functools.partial(_attn_kernel, tq=tq, tk=tk),
        grid=(heads, s // tq),
        in_specs=[pl.BlockSpec((tq, HEAD_DIM), lambda h, i: (i, h)),
                  pl.BlockSpec((s, HEAD_DIM), lambda h, i: (0, heads + h)),
                  pl.BlockSpec((HEAD_DIM, s), lambda h, i: (h, 0)),
                  pl.BlockSpec((None, 1, s), lambda h, i: (h, 0, 0)),
                  pl.BlockSpec((tq, HEAD_DIM), lambda h, i: (i, h))],
        out_specs=pl.BlockSpec((tq, HEAD_DIM), lambda h, i: (i, h)),
        out_shape=jax.ShapeDtypeStruct((s, heads * HEAD_DIM), BF16),
        scratch_shapes=[pltpu.VMEM((s, LANES), BF16),
                        pltpu.VMEM((s, LANES), BF16),
                        pltpu.VMEM((VA_ROWS, s), BF16),
                        pltpu.VMEM((tq, HEAD_DIM + LANES), BF16),
                        pltpu.VMEM((tk, tq), F32),
                        pltpu.VMEM((tk, tq), F32),
                        pltpu.VMEM((1, tq), F32),
                        pltpu.VMEM((VA_ROWS, tq), F32)],
        compiler_params=_params(("parallel", "arbitrary"), 48),
        name="attn",
    )(qk, qk, vt, ft, z)


def _rnn_kernel(x_ref, z_ref, cw_ref, cb_ref, wax_ref, ba_ref, bx_ref, lam_ref, o_ref,
                xpad_ref, hc_ref, a_ref, b_ref, *, tt, wt):
    t = pl.program_id(1)
    groups = tt // SUBLANES

    @pl.when(t == 0)
    def _():
        xpad_ref[:SUBLANES, :] = jnp.zeros((SUBLANES, wt), F32)
        hc_ref[...] = jnp.zeros(hc_ref.shape, F32)

    x = x_ref[...]
    xpad_ref[SUBLANES:, :] = x
    xc = cb_ref[...] + x * cw_ref[CONV_W - 1:CONV_W, :]
    for j in range(1, CONV_W):
        xc = xc + xpad_ref[pl.ds(SUBLANES - j, tt), :] * cw_ref[CONV_W - 1 - j:CONV_W - j, :]
    xpad_ref[:SUBLANES, :] = x[tt - SUBLANES:, :]

    xcb = xc.astype(BF16)
    ga, gx = [], []
    for n in range(wt // LANES):
        g = jnp.dot(xcb[:, n * LANES:(n + 1) * LANES], wax_ref[n], preferred_element_type=F32)
        ga.append(g[:, :LANES])
        gx.append(g[:, LANES:])
    r_gate = _sigmoid(jnp.concatenate(ga, axis=1) + ba_ref[...])
    i_gate = _sigmoid(jnp.concatenate(gx, axis=1) + bx_ref[...])
    log_a = (-RG_C) * r_gate * _softplus(-lam_ref[...])
    a = jnp.exp(log_a)
    mult = jnp.sqrt(-jnp.tanh(log_a) * (a * a + 1.0))
    b = mult * (i_gate * xc)

    a3 = a.reshape(groups, SUBLANES, wt)
    b3 = b.reshape(groups, SUBLANES, wt)
    sub = lax.broadcasted_iota(jnp.int32, (groups, SUBLANES, wt), 1)
    d = 1
    while d < SUBLANES:
        keep = sub >= d
        b3 = a3 * jnp.where(keep, pltpu.roll(b3, d, 1), 0.0) + b3
        a3 = a3 * jnp.where(keep, pltpu.roll(a3, d, 1), 1.0)
        d *= 2
    a_ref[...] = a3.reshape(tt, wt)
    b_ref[...] = b3.reshape(tt, wt)

    def body(g, hprev):
        g0 = pl.multiple_of(g * SUBLANES, SUBLANES)
        hg = b_ref[pl.ds(g0, SUBLANES), :] + a_ref[pl.ds(g0, SUBLANES), :] * hprev
        b_ref[pl.ds(g0, SUBLANES), :] = hg
        return jnp.broadcast_to(hg[SUBLANES - 1:SUBLANES, :], (SUBLANES, wt))

    hlast = lax.fori_loop(0, groups, body, hc_ref[...], unroll=8)
    hc_ref[...] = hlast
    o_ref[...] = (b_ref[...] * z_ref[...].astype(F32)).astype(o_ref.dtype)


def _rnn(x_r, z_r, conv_w, conv_b, w_ax, b_a, b_x, lam, *, tt=1024, wt=512):
    s, w = x_r.shape
    nb = wt // LANES
    tile = pl.BlockSpec((tt, wt), lambda n, t: (t, n))
    vec = pl.BlockSpec((1, wt), lambda n, t: (0, n))
    return pl.pallas_call(
        functools.partial(_rnn_kernel, tt=tt, wt=wt),
        grid=(w // wt, s // tt),
        in_specs=[tile, tile,
                  pl.BlockSpec((CONV_W, wt), lambda n, t: (0, n)), vec,
                  pl.BlockSpec((nb, LANES, 2 * LANES), lambda n, t: (n, 0, 0)),
                  vec, vec, vec],
        out_specs=tile,
        out_shape=jax.ShapeDtypeStruct((s, w), BF16),
        scratch_shapes=[pltpu.VMEM((SUBLANES + tt, wt), F32),
                        pltpu.VMEM((SUBLANES, wt), F32),
                        pltpu.VMEM((tt, wt), F32),
                        pltpu.VMEM((tt, wt), F32)],
        compiler_params=_params(("parallel", "arbitrary"), 40),
        name="rnn",
    )(x_r, z_r, conv_w, conv_b.reshape(1, w), w_ax, b_a.reshape(1, w), b_x.reshape(1, w),
      lam.reshape(1, w))


def _merge_kernel(ya_ref, yr_ref, ga_ref, gr_ref, wba_ref, wbr_ref, o_ref, wa_ref, wr_ref):
    @pl.when(pl.program_id(1) == 0)
    def _():
        _cast_weight_tile(wba_ref, wa_ref)
        _cast_weight_tile(wbr_ref, wr_ref)

    y_a = jnp.dot(ya_ref[...], wa_ref[...], preferred_element_type=F32)
    acc = ga_ref[...].astype(F32) * y_a
    y_r = jnp.dot(yr_ref[...], wr_ref[...], preferred_element_type=F32)
    o_ref[...] = (acc + gr_ref[...].astype(F32) * y_r).astype(o_ref.dtype)


def _merge(ya, yr, g, w_ba, w_br, *, tm=1024, tn=512):
    s, wd = ya.shape
    d = w_ba.shape[1]
    nj = d // tn
    return pl.pallas_call(
        _merge_kernel,
        grid=(d // tn, s // tm),
        in_specs=[pl.BlockSpec((tm, wd), lambda j, i: (i, 0)),
                  pl.BlockSpec((tm, wd), lambda j, i: (i, 0)),
                  pl.BlockSpec((tm, tn), lambda j, i: (i, j)),
                  pl.BlockSpec((tm, tn), lambda j, i: (i, nj + j)),
                  pl.BlockSpec((wd, tn), lambda j, i: (0, j)),
                  pl.BlockSpec((wd, tn), lambda j, i: (0, j))],
        out_specs=pl.BlockSpec((tm, tn), lambda j, i: (i, j)),
        out_shape=jax.ShapeDtypeStruct((s, d), BF16),
        scratch_shapes=[pltpu.VMEM((wd, tn), BF16), pltpu.VMEM((wd, tn), BF16)],
        compiler_params=_params(("parallel", "arbitrary"), 56),
        name="merge",
    )(ya, yr, g, g, w_ba, w_br)


def kernel(x, c, w_ada, b_ada, norm_w, w_in, b_f, q_norm_w, k_norm_w, conv_w, conv_b, w_rg_a,
           b_rg_a, w_rg_x, b_rg_x, lru_lambda, w_br_attn, w_br_rnn, w_gate, b_gate, w_out):
    bsz, s, d = x.shape
    assert bsz == 1
    aw = w_br_attn.shape[0]
    rw = w_br_rnn.shape[0]
    heads = aw // HEAD_DIM
    x2 = x.reshape(s, d)

    mod = _ada(c, w_ada, b_ada)
    shift, scale, gate_res = mod[:, :d], mod[:, d:2 * d], mod[:, 2 * d:]
    h = _prenorm(x2, norm_w, scale, shift)

    o_f = 4 * aw
    o_xr = o_f + heads
    qk_gain = jnp.concatenate([jnp.tile(q_norm_w * (LOG2E / math.sqrt(HEAD_DIM)), heads),
                               jnp.tile(k_norm_w, heads)])
    proj = functools.partial(_proj_stream, h, w_in.T, w_transposed=True, tm=1024, tn=1024)
    qk = proj(n=2 * aw, col0=0, out_dtype=BF16, epilogue="headnorm", row_vecs=(qk_gain,),
              name="proj_qk")
    vt = proj(n=aw, col0=2 * aw, out_dtype=BF16, epilogue="plain", out_transposed=True,
              name="proj_v")
    z_a = proj(n=aw, col0=3 * aw, out_dtype=BF16, epilogue="silu", name="proj_za")
    logf = _proj(h, w_in.T, w_transposed=True, n=LANES, col0=o_f, out_dtype=F32,
                 epilogue="logsigmoid", tm=1024, tn=LANES,
                 row_vecs=(jnp.pad(b_f, (0, LANES - heads)),), name="proj_f")
    x_r = proj(n=rw, col0=o_xr, out_dtype=F32, epilogue="plain", name="proj_xr")
    z_r = proj(n=rw, col0=o_xr + rw, out_dtype=BF16, epilogue="silu", name="proj_zr")

    ft = _cumsum_t(logf)[:heads].reshape(heads, 1, s)
    ya = _attention(qk, vt, z_a, ft, heads=heads)

    w_ax = jnp.concatenate([w_rg_a, w_rg_x], axis=2).astype(BF16)
    yr = _rnn(x_r, z_r, conv_w, conv_b, w_ax, b_rg_a, b_rg_x, lru_lambda)

    g = _proj_stream(h, w_gate, n=2 * d, col0=0, out_dtype=BF16, epilogue="sigmoid_bias", tm=1024,
                     tn=1024, row_vecs=(b_gate,), name="gate")
    merged = _merge(ya, yr, g, w_br_attn, w_br_rnn)
    y = _proj(merged, w_out, n=d, col0=0, out_dtype=x.dtype, epilogue="residual", tm=1024, tn=512,
              tiles=(x2,), row_vecs=(gate_res.reshape(-1),), name="out")
    return y.reshape(bsz, s, d)
```

```python
import functools
import math

import jax
import jax.numpy as jnp
from jax import lax
from jax.experimental import pallas as pl
from jax.experimental.pallas import tpu as pltpu

F32 = jnp.float32
BF16 = jnp.bfloat16

LANES = 128
SUBLANES = 8
HEAD_DIM = 128
CONV_W = 4
RG_C = 8.0
EPS = 1e-6
MASK_VALUE = -1e30
LOG2E = 1.4426950408889634
MIB = 1024 * 1024


def _params(semantics, vmem_mib, flags=None):
    return pltpu.CompilerParams(dimension_semantics=semantics,
                                vmem_limit_bytes=vmem_mib * MIB, flags=flags)


def _sigmoid(x):
    return 0.5 * jnp.tanh(0.5 * x) + 0.5


def _silu(x):
    return x * _sigmoid(x)


def _softplus(x):
    return jnp.maximum(x, 0.0) + jnp.log1p(jnp.exp(-jnp.abs(x)))


def _ada_kernel(c_ref, w_ref, b_ref, o_ref, *, rows):
    d = w_ref.shape[0]

    def body(r, acc):
        r0 = pl.multiple_of(r * rows, rows)
        c = c_ref[pl.ds(r0, rows), :]
        return acc + jnp.sum(w_ref[pl.ds(r0, rows), :] * _silu(c), axis=0, keepdims=True)

    acc = lax.fori_loop(0, d // rows, body, jnp.zeros(o_ref.shape, F32))
    o_ref[...] = acc + b_ref[...]


def _ada(c, w_ada, b_ada, *, tn=512, rows=512):
    d, n = w_ada.shape
    return pl.pallas_call(
        functools.partial(_ada_kernel, rows=rows),
        grid=(n // tn,),
        in_specs=[pl.BlockSpec((d, 1), lambda j: (0, 0)),
                  pl.BlockSpec((d, tn), lambda j: (0, j)),
                  pl.BlockSpec((1, tn), lambda j: (0, j))],
        out_specs=pl.BlockSpec((1, tn), lambda j: (0, j)),
        out_shape=jax.ShapeDtypeStruct((1, n), F32),
        compiler_params=_params(("parallel",), 40),
        name="ada",
    )(c.reshape(d, 1), w_ada, b_ada.reshape(1, n))


def _prenorm_kernel(x_ref, nw_ref, scale_ref, shift_ref, o_ref):
    x = x_ref[...]
    ms = jnp.mean(x * x, axis=-1, keepdims=True)
    h = x * lax.rsqrt(ms + EPS) * nw_ref[...]
    o_ref[...] = (h * (1.0 + scale_ref[...]) + shift_ref[...]).astype(o_ref.dtype)


def _prenorm(x, norm_w, scale, shift, *, tm=512):
    s, d = x.shape
    vec = pl.BlockSpec((1, d), lambda i: (0, 0))
    return pl.pallas_call(
        _prenorm_kernel,
        grid=(s // tm,),
        in_specs=[pl.BlockSpec((tm, d), lambda i: (i, 0)), vec, vec, vec],
        out_specs=pl.BlockSpec((tm, d), lambda i: (i, 0)),
        out_shape=jax.ShapeDtypeStruct((s, d), BF16),
        compiler_params=_params(("parallel",), 40),
        name="prenorm",
    )(x, norm_w.reshape(1, d), scale, shift)


def _cast_weight_tile(w_ref, wb_ref, *, rows=128):
    n_rows = w_ref.shape[0]

    def body(r, carry):
        r0 = pl.multiple_of(r * rows, rows)
        wb_ref[pl.ds(r0, rows), :] = w_ref[pl.ds(r0, rows), :].astype(BF16)
        return carry

    lax.fori_loop(0, n_rows // rows, body, 0)


def _epilogue(y, rest, o_ref, epilogue):
    if epilogue == "plain":
        o_ref[...] = y.astype(o_ref.dtype)
    elif epilogue == "silu":
        o_ref[...] = _silu(y).astype(o_ref.dtype)
    elif epilogue == "sigmoid_bias":
        o_ref[...] = _sigmoid(y + rest[0][...]).astype(o_ref.dtype)
    elif epilogue == "headnorm":
        gain_ref = rest[0]
        for hh in range(y.shape[1] // HEAD_DIM):
            sl = slice(hh * HEAD_DIM, (hh + 1) * HEAD_DIM)
            yh = y[:, sl]
            ms = jnp.mean(yh * yh, axis=-1, keepdims=True)
            o_ref[:, sl] = (yh * lax.rsqrt(ms + EPS) * gain_ref[:, sl]).astype(o_ref.dtype)
    elif epilogue == "logsigmoid":
        z = y + rest[0][...]
        o_ref[...] = (jnp.minimum(z, 0.0) - jnp.log1p(jnp.exp(-jnp.abs(z)))).astype(o_ref.dtype)
    elif epilogue == "residual":
        x_ref, g_ref = rest[0], rest[1]
        o_ref[...] = (x_ref[...] + g_ref[...] * y).astype(o_ref.dtype)
    else:
        raise ValueError(epilogue)


def _proj_kernel(a_ref, w_ref, *rest, epilogue, w_transposed, out_transposed):
    wb_ref = rest[-1]
    o_ref = rest[-2]

    @pl.when(pl.program_id(1) == 0)
    def _():
        _cast_weight_tile(w_ref, wb_ref)

    nt_dims = (((1,), (1,)), ((), ()))
    if out_transposed:
        y = lax.dot_general(wb_ref[...], a_ref[...], nt_dims, preferred_element_type=F32)
    elif w_transposed:
        y = lax.dot_general(a_ref[...], wb_ref[...], nt_dims, preferred_element_type=F32)
    else:
        y = jnp.dot(a_ref[...], wb_ref[...], preferred_element_type=F32)
    _epilogue(y, rest[:-2], o_ref, epilogue)


def _proj(a, w, *, n, col0, out_dtype, epilogue, tm, tn, w_transposed=False, out_transposed=False,
          row_vecs=(), tiles=(), name):
    m, k = a.shape
    assert not out_transposed or (w_transposed and not row_vecs and not tiles)
    assert n % tn == 0 and m % tm == 0
    if w_transposed:
        assert col0 % SUBLANES == 0
        w_spec = pl.BlockSpec((pl.Element(tn), pl.Element(k)),
                              lambda j, i: (pl.multiple_of(col0 + j * tn, SUBLANES), 0))
        wb_shape = (tn, k)
    else:
        assert col0 % tn == 0
        w_spec = pl.BlockSpec((k, tn), lambda j, i: (0, col0 // tn + j))
        wb_shape = (k, tn)
    in_specs = [pl.BlockSpec((tm, k), lambda j, i: (i, 0)), w_spec]
    args = [a, w]
    for t in tiles:
        in_specs.append(pl.BlockSpec((tm, tn), lambda j, i: (i, j)))
        args.append(t)
    for v in row_vecs:
        in_specs.append(pl.BlockSpec((1, tn), lambda j, i: (0, j)))
        args.append(v.reshape(1, n))
    if out_transposed:
        out_spec = pl.BlockSpec((tn, tm), lambda j, i: (j, i))
        out_shape = jax.ShapeDtypeStruct((n, m), out_dtype)
    else:
        out_spec = pl.BlockSpec((tm, tn), lambda j, i: (i, j))
        out_shape = jax.ShapeDtypeStruct((m, n), out_dtype)
    return pl.pallas_call(
        functools.partial(_proj_kernel, epilogue=epilogue, w_transposed=w_transposed,
                          out_transposed=out_transposed),
        grid=(n // tn, m // tm),
        in_specs=in_specs,
        out_specs=out_spec,
        out_shape=out_shape,
        scratch_shapes=[pltpu.VMEM(wb_shape, BF16)],
        compiler_params=_params(("parallel", "arbitrary"), 56),
        name=name,
    )(*args)


W_CHUNKS = 4


def _proj_stream_kernel(a_ref, wc_ref, *rest, epilogue, w_transposed, out_transposed, ni):
    wb_ref = rest[-1]
    o_ref = rest[-2]
    s = pl.program_id(0)
    rows = wc_ref.shape[0]
    u = jnp.maximum(s - W_CHUNKS, 0)
    j = u // ni
    chunk = jnp.minimum(u % ni, W_CHUNKS - 1)

    def cast_chunk(slot, c):
        r0 = pl.multiple_of(c * rows, rows)
        wb_ref[slot, pl.ds(r0, rows), :] = wc_ref[...].astype(BF16)

    @pl.when(s < W_CHUNKS)
    def _():
        cast_chunk(0, s)

    @pl.when(s >= W_CHUNKS)
    def _():
        wb = wb_ref[j % 2]
        nt_dims = (((1,), (1,)), ((), ()))
        if out_transposed:
            y = lax.dot_general(wb, a_ref[...], nt_dims, preferred_element_type=F32)
        elif w_transposed:
            y = lax.dot_general(a_ref[...], wb, nt_dims, preferred_element_type=F32)
        else:
            y = jnp.dot(a_ref[...], wb, preferred_element_type=F32)
        _epilogue(y, rest[:-2], o_ref, epilogue)
        cast_chunk((j + 1) % 2, chunk)


def _proj_stream(a, w, *, n, col0, out_dtype, epilogue, tm, tn, w_transposed=False,
                 out_transposed=False, row_vecs=(), name):
    m, k = a.shape
    nj, ni = n // tn, m // tm
    assert n % tn == 0 and m % tm == 0 and ni >= W_CHUNKS
    assert not out_transposed or (w_transposed and not row_vecs)

    def where_to(s):
        u = jnp.maximum(s - W_CHUNKS, 0)
        j, i = u // ni, u % ni
        pro = s < W_CHUNKS
        last = j == nj - 1
        tile = jnp.where(pro, 0, jnp.minimum(j + 1, nj - 1))
        chunk = jnp.where(pro, s, jnp.where(last, W_CHUNKS - 1, jnp.minimum(i, W_CHUNKS - 1)))
        return tile, chunk, j, i

    if w_transposed:
        assert col0 % SUBLANES == 0 and tn % W_CHUNKS == 0
        cr = tn // W_CHUNKS

        def w_index(s):
            tile, chunk, _, _ = where_to(s)
            return pl.multiple_of(col0 + tile * tn + chunk * cr, SUBLANES), 0

        w_spec = pl.BlockSpec((pl.Element(cr), pl.Element(k)), w_index)
        wb_shape = (2, tn, k)
    else:
        assert col0 % tn == 0 and k % W_CHUNKS == 0
        ck = k // W_CHUNKS

        def w_index(s):
            tile, chunk, _, _ = where_to(s)
            return chunk, col0 // tn + tile

        w_spec = pl.BlockSpec((ck, tn), w_index)
        wb_shape = (2, k, tn)

    def a_index(s):
        _, _, _, i = where_to(s)
        return i, 0

    def vec_index(s):
        _, _, j, _ = where_to(s)
        return 0, j

    def out_index(s):
        _, _, j, i = where_to(s)
        return (j, i) if out_transposed else (i, j)

    in_specs = [pl.BlockSpec((tm, k), a_index), w_spec]
    args = [a, w]
    for v in row_vecs:
        in_specs.append(pl.BlockSpec((1, tn), vec_index))
        args.append(v.reshape(1, n))
    if out_transposed:
        out_spec = pl.BlockSpec((tn, tm), out_index)
        out_shape = jax.ShapeDtypeStruct((n, m), out_dtype)
    else:
        out_spec = pl.BlockSpec((tm, tn), out_index)
        out_shape = jax.ShapeDtypeStruct((m, n), out_dtype)
    return pl.pallas_call(
        functools.partial(_proj_stream_kernel, epilogue=epilogue, w_transposed=w_transposed,
                          out_transposed=out_transposed, ni=ni),
        grid=(W_CHUNKS + nj * ni,),
        in_specs=in_specs,
        out_specs=out_spec,
        out_shape=out_shape,
        scratch_shapes=[pltpu.VMEM(wb_shape, BF16)],
        compiler_params=_params(("arbitrary",), 56),
        name=name,
    )(*args)


def _cumsum_kernel(x_ref, ft_ref, *, chunk):
    s = x_ref.shape[0]
    row = lax.broadcasted_iota(jnp.int32, (chunk, LANES), 0)

    def body(c, carry):
        s0 = pl.multiple_of(c * chunk, chunk)
        x = x_ref[pl.ds(s0, chunk), :]
        d = 1
        while d < chunk:
            x = x + jnp.where(row >= d, pltpu.roll(x, d, 0), 0.0)
            d *= 2
        x = x + carry
        ft_ref[:, pl.ds(s0, chunk)] = x.T
        return x[chunk - 1:chunk, :]

    lax.fori_loop(0, s // chunk, body, jnp.zeros((1, LANES), F32))


def _cumsum_t(logf, *, chunk=512):
    s = logf.shape[0]
    return pl.pallas_call(
        functools.partial(_cumsum_kernel, chunk=chunk),
        out_shape=jax.ShapeDtypeStruct((LANES, s), F32),
        compiler_params=_params(None, 40),
        name="cumsum",
    )(logf)


N_SPLIT = 3


def _decay_features(ft_ref, qf_ref, kf_ref, *, chunk=512):
    s = ft_ref.shape[1]
    row = lax.broadcasted_iota(jnp.int32, (LANES, chunk), 0)

    def body(c, carry):
        c0 = pl.multiple_of(c * chunk, chunk)
        rem = ft_ref[:, pl.ds(c0, chunk)] * LOG2E
        qf = jnp.where((row >= N_SPLIT) & (row < 2 * N_SPLIT), 1.0, 0.0)
        kf = jnp.where(row < N_SPLIT, 1.0, 0.0)
        for p in range(N_SPLIT):
            piece = rem.astype(BF16).astype(F32)
            rem = rem - piece
            qf = jnp.where(row == p, piece, qf)
            kf = jnp.where(row == N_SPLIT + p, -piece, kf)
        qf_ref[pl.ds(c0, chunk), :] = qf.T.astype(BF16)
        kf_ref[pl.ds(c0, chunk), :] = kf.T.astype(BF16)
        return carry

    lax.fori_loop(0, s // chunk, body, 0)


VA_ROWS = HEAD_DIM + 16
QUERY_CHUNK = 256


def _attn_kernel(q_ref, k_ref, vt_ref, ft_ref, z_ref, o_ref, qf_ref, kf_ref, vat_ref, qa_ref, sa_ref,
                 sb_ref, m_ref, acc_ref, *, tq, tk):
    i = pl.program_id(1)
    nt_dims = (((1,), (1,)), ((), ()))

    @pl.when(i == 0)
    def _():
        _decay_features(ft_ref, qf_ref, kf_ref)
        vat_ref[:HEAD_DIM, :] = vt_ref[...]
        row = lax.broadcasted_iota(jnp.int32, (VA_ROWS - HEAD_DIM, vat_ref.shape[1]), 0)
        vat_ref[HEAD_DIM:, :] = jnp.where(row == 0, 1.0, 0.0).astype(BF16)

    q0 = pl.multiple_of(i * tq, tq)
    qa_ref[:, :HEAD_DIM] = q_ref[...]
    qa_ref[:, HEAD_DIM:] = qf_ref[pl.ds(q0, tq), :]

    m_ref[...] = jnp.full(m_ref.shape, MASK_VALUE, F32)
    acc_ref[...] = jnp.zeros(acc_ref.shape, F32)

    def scores(j, dst_ref, c0=0):
        k0 = pl.multiple_of(j * tk, tk)
        ka = jnp.concatenate([k_ref[pl.ds(k0, tk), :], kf_ref[pl.ds(k0, tk), :]], axis=1)
        dst_ref[:, c0:] = lax.dot_general(ka, qa_ref[c0:, :], nt_dims,
                                          preferred_element_type=F32)

    def consume(src_ref, j, c0=0, causal=False):
        k0 = pl.multiple_of(j * tk, tk)
        row0 = pl.multiple_of(jnp.minimum(i, 0) * SUBLANES, SUBLANES)
        for q1 in range(c0, tq, QUERY_CHUNK):
            qs = slice(q1, q1 + QUERY_CHUNK)
            s2 = src_ref[pl.ds(row0, tk), qs]
            if causal and q1 - c0 < tk:
                keys = lax.broadcasted_iota(jnp.int32, (tk, QUERY_CHUNK), 0)
                qrys = lax.broadcasted_iota(jnp.int32, (tk, QUERY_CHUNK), 1) + (q1 - c0)
                s2 = jnp.where(qrys >= keys, s2, MASK_VALUE)
            m_prev = m_ref[:, qs]
            m_new = jnp.maximum(m_prev, jnp.max(s2, axis=0, keepdims=True))
            alpha = jnp.exp2(m_prev - m_new)
            p = jnp.exp2((s2 - m_new).astype(BF16))
            pv = jnp.dot(vat_ref[:, pl.ds(k0, tk)], p, preferred_element_type=F32)
            acc_ref[:, qs] = alpha * acc_ref[:, qs] + pv
            m_ref[:, qs] = m_new

    nk = tq // tk
    scores(0, sa_ref)

    def pair(jj):
        j = 2 * jj
        scores(j + 1, sb_ref)
        consume(sa_ref, j)
        scores(j + 2, sa_ref)
        consume(sb_ref, j + 1)

    def quad(qq, carry):
        pair(2 * qq)
        pair(2 * qq + 1)
        return carry

    n_pairs = i * (nk // 2)
    lax.fori_loop(0, n_pairs // 2, quad, 0)

    @pl.when(n_pairs % 2 == 1)
    def _():
        pair(n_pairs - 1)

    bufs = (sa_ref, sb_ref)
    for d in range(nk):
        if d + 1 < nk:
            scores(nk * i + d + 1, bufs[(d + 1) % 2], c0=(d + 1) * tk)
        consume(bufs[d % 2], nk * i + d, c0=d * tk, causal=True)

    acc = acc_ref[...]
    att = (acc[:HEAD_DIM, :] / acc[HEAD_DIM:HEAD_DIM + 1, :]).T
    o_ref[...] = (att * z_ref[...].astype(F32)).astype(o_ref.dtype)


def _attention(qk, vt, z, ft, *, heads, tq=1024, tk=512):
    s = qk.shape[0]
    assert tq % (2 * tk) == 0 and s % tq == 0
    return pl.pallas_call(
        functools.partial(_attn_kernel, tq=tq, tk=tk),
        grid=(heads, s // tq),
        in_specs=[pl.BlockSpec((tq, HEAD_DIM), lambda h, i: (i, h)),
                  pl.BlockSpec((s, HEAD_DIM), lambda h, i: (0, heads + h)),
                  pl.BlockSpec((HEAD_DIM, s), lambda h, i: (h, 0)),
                  pl.BlockSpec((None, 1, s), lambda h, i: (h, 0, 0)),
                  pl.BlockSpec((tq, HEAD_DIM), lambda h, i: (i, h))],
        out_specs=pl.BlockSpec((tq, HEAD_DIM), lambda h, i: (i, h)),
        out_shape=jax.ShapeDtypeStruct((s, heads * HEAD_DIM), BF16),
        scratch_shapes=[pltpu.VMEM((s, LANES), BF16),
                        pltpu.VMEM((s, LANES), BF16),
                        pltpu.VMEM((VA_ROWS, s), BF16),
                        pltpu.VMEM((tq, HEAD_DIM + LANES), BF16),
                        pltpu.VMEM((tk, tq), F32),
                        pltpu.VMEM((tk, tq), F32),
                        pltpu.VMEM((1, tq), F32),
                        pltpu.VMEM((VA_ROWS, tq), F32)],
        compiler_params=_params(("parallel", "arbitrary"), 48),
        name="attn",
    )(qk, qk, vt, ft, z)


def _rnn_kernel(x_ref, z_ref, cw_ref, cb_ref, wax_ref, ba_ref, bx_ref, lam_ref, o_ref,
                xpad_ref, hc_ref, a_ref, b_ref, *, tt, wt):
    t = pl.program_id(1)
    groups = tt // SUBLANES

    @pl.when(t == 0)
    def _():
        xpad_ref[:SUBLANES, :] = jnp.zeros((SUBLANES, wt), F32)
        hc_ref[...] = jnp.zeros(hc_ref.shape, F32)

    x = x_ref[...]
    xpad_ref[SUBLANES:, :] = x
    xc = cb_ref[...] + x * cw_ref[CONV_W - 1:CONV_W, :]
    for j in range(1, CONV_W):
        xc = xc + xpad_ref[pl.ds(SUBLANES - j, tt), :] * cw_ref[CONV_W - 1 - j:CONV_W - j, :]
    xpad_ref[:SUBLANES, :] = x[tt - SUBLANES:, :]

    xcb = xc.astype(BF16)
    ga, gx = [], []
    for n in range(wt // LANES):
        g = jnp.dot(xcb[:, n * LANES:(n + 1) * LANES], wax_ref[n], preferred_element_type=F32)
        ga.append(g[:, :LANES])
        gx.append(g[:, LANES:])
    r_gate = _sigmoid(jnp.concatenate(ga, axis=1) + ba_ref[...])
    i_gate = _sigmoid(jnp.concatenate(gx, axis=1) + bx_ref[...])
    log_a = (-RG_C) * r_gate * _softplus(-lam_ref[...])
    a = jnp.exp(log_a)
    mult = jnp.sqrt(-jnp.tanh(log_a) * (a * a + 1.0))
    b = mult * (i_gate * xc)

    a3 = a.reshape(groups, SUBLANES, wt)
    b3 = b.reshape(groups, SUBLANES, wt)
    sub = lax.broadcasted_iota(jnp.int32, (groups, SUBLANES, wt), 1)
    d = 1
    while d < SUBLANES:
        keep = sub >= d
        b3 = a3 * jnp.where(keep, pltpu.roll(b3, d, 1), 0.0) + b3
        a3 = a3 * jnp.where(keep, pltpu.roll(a3, d, 1), 1.0)
        d *= 2
    a_ref[...] = a3.reshape(tt, wt)
    b_ref[...] = b3.reshape(tt, wt)

    def body(g, hprev):
        g0 = pl.multiple_of(g * SUBLANES, SUBLANES)
        hg = b_ref[pl.ds(g0, SUBLANES), :] + a_ref[pl.ds(g0, SUBLANES), :] * hprev
        b_ref[pl.ds(g0, SUBLANES), :] = hg
        return jnp.broadcast_to(hg[SUBLANES - 1:SUBLANES, :], (SUBLANES, wt))

    hlast = lax.fori_loop(0, groups, body, hc_ref[...], unroll=8)
    hc_ref[...] = hlast
    o_ref[...] = (b_ref[...] * z_ref[...].astype(F32)).astype(o_ref.dtype)


def _rnn(x_r, z_r, conv_w, conv_b, w_ax, b_a, b_x, lam, *, tt=1024, wt=512):
    s, w = x_r.shape
    nb = wt // LANES
    tile = pl.BlockSpec((tt, wt), lambda n, t: (t, n))
    vec = pl.BlockSpec((1, wt), lambda n, t: (0, n))
    return pl.pallas_call(
        functools.partial(_rnn_kernel, tt=tt, wt=wt),
        grid=(w // wt, s // tt),
        in_specs=[tile, tile,
                  pl.BlockSpec((CONV_W, wt), lambda n, t: (0, n)), vec,
                  pl.BlockSpec((nb, LANES, 2 * LANES), lambda n, t: (n, 0, 0)),
                  vec, vec, vec],
        out_specs=tile,
        out_shape=jax.ShapeDtypeStruct((s, w), BF16),
        scratch_shapes=[pltpu.VMEM((SUBLANES + tt, wt), F32),
                        pltpu.VMEM((SUBLANES, wt), F32),
                        pltpu.VMEM((tt, wt), F32),
                        pltpu.VMEM((tt, wt), F32)],
        compiler_params=_params(("parallel", "arbitrary"), 40),
        name="rnn",
    )(x_r, z_r, conv_w, conv_b.reshape(1, w), w_ax, b_a.reshape(1, w), b_x.reshape(1, w),
      lam.reshape(1, w))


def _merge_kernel(ya_ref, yr_ref, ga_ref, gr_ref, wba_ref, wbr_ref, o_ref, wa_ref, wr_ref):
    @pl.when(pl.program_id(1) == 0)
    def _():
        _cast_weight_tile(wba_ref, wa_ref)
        _cast_weight_tile(wbr_ref, wr_ref)

    y_a = jnp.dot(ya_ref[...], wa_ref[...], preferred_element_type=F32)
    acc = ga_ref[...].astype(F32) * y_a
    y_r = jnp.dot(yr_ref[...], wr_ref[...], preferred_element_type=F32)
    o_ref[...] = (acc + gr_ref[...].astype(F32) * y_r).astype(o_ref.dtype)


def _merge(ya, yr, g, w_ba, w_br, *, tm=1024, tn=512):
    s, wd = ya.shape
    d = w_ba.shape[1]
    nj = d // tn
    return pl.pallas_call(
        _merge_kernel,
        grid=(d // tn, s // tm),
        in_specs=[pl.BlockSpec((tm, wd), lambda j, i: (i, 0)),
                  pl.BlockSpec((tm, wd), lambda j, i: (i, 0)),
                  pl.BlockSpec((tm, tn), lambda j, i: (i, j)),
                  pl.BlockSpec((tm, tn), lambda j, i: (i, nj + j)),
                  pl.BlockSpec((wd, tn), lambda j, i: (0, j)),
                  pl.BlockSpec((wd, tn), lambda j, i: (0, j))],
        out_specs=pl.BlockSpec((tm, tn), lambda j, i: (i, j)),
        out_shape=jax.ShapeDtypeStruct((s, d), BF16),
        scratch_shapes=[pltpu.VMEM((wd, tn), BF16), pltpu.VMEM((wd, tn), BF16)],
        compiler_params=_params(("parallel", "arbitrary"), 56),
        name="merge",
    )(ya, yr, g, g, w_ba, w_br)


def kernel(x, c, w_ada, b_ada, norm_w, w_in, b_f, q_norm_w, k_norm_w, conv_w, conv_b, w_rg_a,
           b_rg_a, w_rg_x, b_rg_x, lru_lambda, w_br_attn, w_br_rnn, w_gate, b_gate, w_out):
    bsz, s, d = x.shape
    assert bsz == 1
    aw = w_br_attn.shape[0]
    rw = w_br_rnn.shape[0]
    heads = aw // HEAD_DIM
    x2 = x.reshape(s, d)

    mod = _ada(c, w_ada, b_ada)
    shift, scale, gate_res = mod[:, :d], mod[:, d:2 * d], mod[:, 2 * d:]
    h = _prenorm(x2, norm_w, scale, shift)

    o_f = 4 * aw
    o_xr = o_f + heads
    qk_gain = jnp.concatenate([jnp.tile(q_norm_w * (LOG2E / math.sqrt(HEAD_DIM)), heads),
                               jnp.tile(k_norm_w, heads)])
    proj = functools.partial(_proj_stream, h, w_in.T, w_transposed=True, tm=1024, tn=1024)
    qk = proj(n=2 * aw, col0=0, out_dtype=BF16, epilogue="headnorm", row_vecs=(qk_gain,),
              name="proj_qk")
    vt = proj(n=aw, col0=2 * aw, out_dtype=BF16, epilogue="plain", out_transposed=True,
              name="proj_v")
    z_a = proj(n=aw, col0=3 * aw, out_dtype=BF16, epilogue="silu", name="proj_za")
    logf = _proj(h, w_in.T, w_transposed=True, n=LANES, col0=o_f, out_dtype=F32,
                 epilogue="logsigmoid", tm=1024, tn=LANES,
                 row_vecs=(jnp.pad(b_f, (0, LANES - heads)),), name="proj_f")
    x_r = proj(n=rw, col0=o_xr, out_dtype=F32, epilogue="plain", name="proj_xr")
    z_r = proj(n=rw, col0=o_xr + rw, out_dtype=BF16, epilogue="silu", name="proj_zr")

    ft = _cumsum_t(logf)[:heads].reshape(heads, 1, s)
    ya = _attention(qk, vt, z_a, ft, heads=heads)

    w_ax = jnp.concatenate([w_rg_a, w_rg_x], axis=2).astype(BF16)
    yr = _rnn(x_r, z_r, conv_w, conv_b, w_ax, b_rg_a, b_rg_x, lru_lambda)

    g = _proj_stream(h, w_gate, n=2 * d, col0=0, out_dtype=BF16, epilogue="sigmoid_bias", tm=1024,
                     tn=1024, row_vecs=(b_gate,), name="gate")
    merged = _merge(ya, yr, g, w_br_attn, w_br_rnn)
    y = _proj(merged, w_out, n=d, col0=0, out_dtype=x.dtype, epilogue="residual", tm=1024, tn=512,
              tiles=(x2,), row_vecs=(gate_res.reshape(-1),), name="out")
    return y.reshape(bsz, s, d)
```

```python
import functools
import math

import jax
import jax.numpy as jnp
from jax import lax
from jax.experimental import pallas as pl
from jax.experimental.pallas import tpu as pltpu

F32 = jnp.float32
BF16 = jnp.bfloat16

LANES = 128
SUBLANES = 8
HEAD_DIM = 128
CONV_W = 4
RG_C = 8.0
EPS = 1e-6
MASK_VALUE = -1e30
LOG2E = 1.4426950408889634
MIB = 1024 * 1024


def _params(semantics, vmem_mib, flags=None):
    return pltpu.CompilerParams(dimension_semantics=semantics,
                                vmem_limit_bytes=vmem_mib * MIB, flags=flags)


def _sigmoid(x):
    return 0.5 * jnp.tanh(0.5 * x) + 0.5


def _silu(x):
    return x * _sigmoid(x)


def _softplus(x):
    return jnp.maximum(x, 0.0) + jnp.log1p(jnp.exp(-jnp.abs(x)))


def _ada_kernel(c_ref, w_ref, b_ref, o_ref, *, rows):
    d = w_ref.shape[0]

    def body(r, acc):
        r0 = pl.multiple_of(r * rows, rows)
        c = c_ref[pl.ds(r0, rows), :]
        return acc + jnp.sum(w_ref[pl.ds(r0, rows), :] * _silu(c), axis=0, keepdims=True)

    acc = lax.fori_loop(0, d // rows, body, jnp.zeros(o_ref.shape, F32))
    o_ref[...] = acc + b_ref[...]


def _ada(c, w_ada, b_ada, *, tn=512, rows=512):
    d, n = w_ada.shape
    return pl.pallas_call(
        functools.partial(_ada_kernel, rows=rows),
        grid=(n // tn,),
        in_specs=[pl.BlockSpec((d, 1), lambda j: (0, 0)),
                  pl.BlockSpec((d, tn), lambda j: (0, j)),
                  pl.BlockSpec((1, tn), lambda j: (0, j))],
        out_specs=pl.BlockSpec((1, tn), lambda j: (0, j)),
        out_shape=jax.ShapeDtypeStruct((1, n), F32),
        compiler_params=_params(("parallel",), 40),
        name="ada",
    )(c.reshape(d, 1), w_ada, b_ada.reshape(1, n))


def _prenorm_kernel(x_ref, nw_ref, scale_ref, shift_ref, o_ref):
    x = x_ref[...]
    ms = jnp.mean(x * x, axis=-1, keepdims=True)
    h = x * lax.rsqrt(ms + EPS) * nw_ref[...]
    o_ref[...] = (h * (1.0 + scale_ref[...]) + shift_ref[...]).astype(o_ref.dtype)


def _prenorm(x, norm_w, scale, shift, *, tm=512):
    s, d = x.shape
    vec = pl.BlockSpec((1, d), lambda i: (0, 0))
    return pl.pallas_call(
        _prenorm_kernel,
        grid=(s // tm,),
        in_specs=[pl.BlockSpec((tm, d), lambda i: (i, 0)), vec, vec, vec],
        out_specs=pl.BlockSpec((tm, d), lambda i: (i, 0)),
        out_shape=jax.ShapeDtypeStruct((s, d), BF16),
        compiler_params=_params(("parallel",), 40),
        name="prenorm",
    )(x, norm_w.reshape(1, d), scale, shift)


def _cast_weight_tile(w_ref, wb_ref, *, rows=128):
    n_rows = w_ref.shape[0]

    def body(r, carry):
        r0 = pl.multiple_of(r * rows, rows)
        wb_ref[pl.ds(r0, rows), :] = w_ref[pl.ds(r0, rows), :].astype(BF16)
        return carry

    lax.fori_loop(0, n_rows // rows, body, 0)


def _epilogue(y, rest, o_ref, epilogue):
    if epilogue == "plain":
        o_ref[...] = y.astype(o_ref.dtype)
    elif epilogue == "silu":
        o_ref[...] = _silu(y).astype(o_ref.dtype)
    elif epilogue == "sigmoid_bias":
        o_ref[...] = _sigmoid(y + rest[0][...]).astype(o_ref.dtype)
    elif epilogue == "headnorm":
        gain_ref = rest[0]
        for hh in range(y.shape[1] // HEAD_DIM):
            sl = slice(hh * HEAD_DIM, (hh + 1) * HEAD_DIM)
            yh = y[:, sl]
            ms = jnp.mean(yh * yh, axis=-1, keepdims=True)
            o_ref[:, sl] = (yh * lax.rsqrt(ms + EPS) * gain_ref[:, sl]).astype(o_ref.dtype)
    elif epilogue == "logsigmoid":
        z = y + rest[0][...]
        o_ref[...] = (jnp.minimum(z, 0.0) - jnp.log1p(jnp.exp(-jnp.abs(z)))).astype(o_ref.dtype)
    elif epilogue == "residual":
        x_ref, g_ref = rest[0], rest[1]
        o_ref[...] = (x_ref[...] + g_ref[...] * y).astype(o_ref.dtype)
    else:
        raise ValueError(epilogue)


def _proj_kernel(a_ref, w_ref, *rest, epilogue, w_transposed, out_transposed):
    wb_ref = rest[-1]
    o_ref = rest[-2]

    @pl.when(pl.program_id(1) == 0)
    def _():
        _cast_weight_tile(w_ref, wb_ref)

    nt_dims = (((1,), (1,)), ((), ()))
    if out_transposed:
        y = lax.dot_general(wb_ref[...], a_ref[...], nt_dims, preferred_element_type=F32)
    elif w_transposed:
        y = lax.dot_general(a_ref[...], wb_ref[...], nt_dims, preferred_element_type=F32)
    else:
        y = jnp.dot(a_ref[...], wb_ref[...], preferred_element_type=F32)
    _epilogue(y, rest[:-2], o_ref, epilogue)


def _proj(a, w, *, n, col0, out_dtype, epilogue, tm, tn, w_transposed=False, out_transposed=False,
          row_vecs=(), tiles=(), name):
    m, k = a.shape
    assert not out_transposed or (w_transposed and not row_vecs and not tiles)
    assert n % tn == 0 and m % tm == 0
    if w_transposed:
        assert col0 % SUBLANES == 0
        w_spec = pl.BlockSpec((pl.Element(tn), pl.Element(k)),
                              lambda j, i: (pl.multiple_of(col0 + j * tn, SUBLANES), 0))
        wb_shape = (tn, k)
    else:
        assert col0 % tn == 0
        w_spec = pl.BlockSpec((k, tn), lambda j, i: (0, col0 // tn + j))
        wb_shape = (k, tn)
    in_specs = [pl.BlockSpec((tm, k), lambda j, i: (i, 0)), w_spec]
    args = [a, w]
    for t in tiles:
        in_specs.append(pl.BlockSpec((tm, tn), lambda j, i: (i, j)))
        args.append(t)
    for v in row_vecs:
        in_specs.append(pl.BlockSpec((1, tn), lambda j, i: (0, j)))
        args.append(v.reshape(1, n))
    if out_transposed:
        out_spec = pl.BlockSpec((tn, tm), lambda j, i: (j, i))
        out_shape = jax.ShapeDtypeStruct((n, m), out_dtype)
    else:
        out_spec = pl.BlockSpec((tm, tn), lambda j, i: (i, j))
        out_shape = jax.ShapeDtypeStruct((m, n), out_dtype)
    return pl.pallas_call(
        functools.partial(_proj_kernel, epilogue=epilogue, w_transposed=w_transposed,
                          out_transposed=out_transposed),
        grid=(n // tn, m // tm),
        in_specs=in_specs,
        out_specs=out_spec,
        out_shape=out_shape,
        scratch_shapes=[pltpu.VMEM(wb_shape, BF16)],
        compiler_params=_params(("parallel", "arbitrary"), 56),
        name=name,
    )(*args)


W_CHUNKS = 4


def _proj_stream_kernel(a_ref, wc_ref, *rest, epilogue, w_transposed, out_transposed, ni, nc):
    wb_ref = rest[-1]
    o_ref = rest[-2]
    s = pl.program_id(0)
    rows = wc_ref.shape[0]
    u = jnp.maximum(s - nc, 0)
    j = u // ni
    chunk = jnp.minimum(u % ni, nc - 1)

    def cast_chunk(slot, c):
        r0 = pl.multiple_of(c * rows, rows)
        wb_ref[slot, pl.ds(r0, rows), :] = wc_ref[...].astype(BF16)

    @pl.when(s < nc)
    def _():
        cast_chunk(0, s)

    @pl.when(s >= nc)
    def _():
        wb = wb_ref[j % 2]
        nt_dims = (((1,), (1,)), ((), ()))
        if out_transposed:
            y = lax.dot_general(wb, a_ref[...], nt_dims, preferred_element_type=F32)
        elif w_transposed:
            y = lax.dot_general(a_ref[...], wb, nt_dims, preferred_element_type=F32)
        else:
            y = jnp.dot(a_ref[...], wb, preferred_element_type=F32)
        _epilogue(y, rest[:-2], o_ref, epilogue)
        cast_chunk((j + 1) % 2, chunk)


def _proj_stream(a, w, *, n, col0, out_dtype, epilogue, tm, tn, w_transposed=False,
                 out_transposed=False, row_vecs=(), tiles=(), nc=W_CHUNKS, vmem_mib=56, name):
    m, k = a.shape
    nj, ni = n // tn, m // tm
    assert n % tn == 0 and m % tm == 0 and ni >= nc
    assert not out_transposed or (w_transposed and not row_vecs and not tiles)

    def where_to(s):
        u = jnp.maximum(s - nc, 0)
        j, i = u // ni, u % ni
        pro = s < nc
        last = j == nj - 1
        tile = jnp.where(pro, 0, jnp.minimum(j + 1, nj - 1))
        chunk = jnp.where(pro, s, jnp.where(last, nc - 1, jnp.minimum(i, nc - 1)))
        return tile, chunk, j, i

    if w_transposed:
        assert col0 % SUBLANES == 0 and tn % nc == 0
        cr = tn // nc

        def w_index(s):
            tile, chunk, _, _ = where_to(s)
            return pl.multiple_of(col0 + tile * tn + chunk * cr, SUBLANES), 0

        w_spec = pl.BlockSpec((pl.Element(cr), pl.Element(k)), w_index)
        wb_shape = (2, tn, k)
    else:
        assert col0 % tn == 0 and k % nc == 0
        ck = k // nc

        def w_index(s):
            tile, chunk, _, _ = where_to(s)
            return chunk, col0 // tn + tile

        w_spec = pl.BlockSpec((ck, tn), w_index)
        wb_shape = (2, k, tn)

    def a_index(s):
        _, _, _, i = where_to(s)
        return i, 0

    def vec_index(s):
        _, _, j, _ = where_to(s)
        return 0, j

    def out_index(s):
        _, _, j, i = where_to(s)
        return (j, i) if out_transposed else (i, j)

    in_specs = [pl.BlockSpec((tm, k), a_index), w_spec]
    args = [a, w]
    for t in tiles:
        in_specs.append(pl.BlockSpec((tm, tn), out_index))
        args.append(t)
    for v in row_vecs:
        in_specs.append(pl.BlockSpec((1, tn), vec_index))
        args.append(v.reshape(1, n))
    if out_transposed:
        out_spec = pl.BlockSpec((tn, tm), out_index)
        out_shape = jax.ShapeDtypeStruct((n, m), out_dtype)
    else:
        out_spec = pl.BlockSpec((tm, tn), out_index)
        out_shape = jax.ShapeDtypeStruct((m, n), out_dtype)
    return pl.pallas_call(
        functools.partial(_proj_stream_kernel, epilogue=epilogue, w_transposed=w_transposed,
                          out_transposed=out_transposed, ni=ni, nc=nc),
        grid=(nc + nj * ni,),
        in_specs=in_specs,
        out_specs=out_spec,
        out_shape=out_shape,
        scratch_shapes=[pltpu.VMEM(wb_shape, BF16)],
        compiler_params=_params(("arbitrary",), vmem_mib),
        name=name,
    )(*args)


def _cumsum_kernel(x_ref, ft_ref, *, chunk):
    s = x_ref.shape[0]
    row = lax.broadcasted_iota(jnp.int32, (chunk, LANES), 0)

    def body(c, carry):
        s0 = pl.multiple_of(c * chunk, chunk)
        x = x_ref[pl.ds(s0, chunk), :]
        d = 1
        while d < chunk:
            x = x + jnp.where(row >= d, pltpu.roll(x, d, 0), 0.0)
            d *= 2
        x = x + carry
        ft_ref[:, pl.ds(s0, chunk)] = x.T
        return x[chunk - 1:chunk, :]

    lax.fori_loop(0, s // chunk, body, jnp.zeros((1, LANES), F32))


def _cumsum_t(logf, *, chunk=512):
    s = logf.shape[0]
    return pl.pallas_call(
        functools.partial(_cumsum_kernel, chunk=chunk),
        out_shape=jax.ShapeDtypeStruct((LANES, s), F32),
        compiler_params=_params(None, 40),
        name="cumsum",
    )(logf)


N_SPLIT = 3


def _decay_features(ft_ref, qf_ref, kf_ref, *, chunk=512):
    s = ft_ref.shape[1]
    row = lax.broadcasted_iota(jnp.int32, (LANES, chunk), 0)

    def body(c, carry):
        c0 = pl.multiple_of(c * chunk, chunk)
        rem = ft_ref[:, pl.ds(c0, chunk)] * LOG2E
        qf = jnp.where((row >= N_SPLIT) & (row < 2 * N_SPLIT), 1.0, 0.0)
        kf = jnp.where(row < N_SPLIT, 1.0, 0.0)
        for p in range(N_SPLIT):
            piece = rem.astype(BF16).astype(F32)
            rem = rem - piece
            qf = jnp.where(row == p, piece, qf)
            kf = jnp.where(row == N_SPLIT + p, -piece, kf)
        qf_ref[pl.ds(c0, chunk), :] = qf.T.astype(BF16)
        kf_ref[pl.ds(c0, chunk), :] = kf.T.astype(BF16)
        return carry

    lax.fori_loop(0, s // chunk, body, 0)


VA_ROWS = HEAD_DIM + 16
QUERY_CHUNK = 256


def _attn_kernel(q_ref, k_ref, vt_ref, ft_ref, z_ref, o_ref, qf_ref, kf_ref, vat_ref, qa_ref, sa_ref,
                 sb_ref, m_ref, acc_ref, *, tq, tk):
    i = pl.program_id(1)
    nt_dims = (((1,), (1,)), ((), ()))

    @pl.when(i == 0)
    def _():
        _decay_features(ft_ref, qf_ref, kf_ref)
        vat_ref[:HEAD_DIM, :] = vt_ref[...]
        row = lax.broadcasted_iota(jnp.int32, (VA_ROWS - HEAD_DIM, vat_ref.shape[1]), 0)
        vat_ref[HEAD_DIM:, :] = jnp.where(row == 0, 1.0, 0.0).astype(BF16)

    q0 = pl.multiple_of(i * tq, tq)
    qa_ref[:, :HEAD_DIM] = q_ref[...]
    qa_ref[:, HEAD_DIM:] = qf_ref[pl.ds(q0, tq), :]

    m_ref[...] = jnp.full(m_ref.shape, MASK_VALUE, F32)
    acc_ref[...] = jnp.zeros(acc_ref.shape, F32)

    def scores(j, dst_ref, c0=0):
        k0 = pl.multiple_of(j * tk, tk)
        ka = jnp.concatenate([k_ref[pl.ds(k0, tk), :], kf_ref[pl.ds(k0, tk), :]], axis=1)
        dst_ref[:, c0:] = lax.dot_general(ka, qa_ref[c0:, :], nt_dims,
                                          preferred_element_type=F32)

    def consume(src_ref, j, c0=0, causal=False):
        k0 = pl.multiple_of(j * tk, tk)
        row0 = pl.multiple_of(jnp.minimum(i, 0) * SUBLANES, SUBLANES)
        for q1 in range(c0, tq, QUERY_CHUNK):
            qs = slice(q1, q1 + QUERY_CHUNK)
            s2 = src_ref[pl.ds(row0, tk), qs]
            if causal and q1 - c0 < tk:
                keys = lax.broadcasted_iota(jnp.int32, (tk, QUERY_CHUNK), 0)
                qrys = lax.broadcasted_iota(jnp.int32, (tk, QUERY_CHUNK), 1) + (q1 - c0)
                s2 = jnp.where(qrys >= keys, s2, MASK_VALUE)
            m_prev = m_ref[:, qs]
            m_new = jnp.maximum(m_prev, jnp.max(s2, axis=0, keepdims=True))
            alpha = jnp.exp2(m_prev - m_new)
            p = jnp.exp2((s2 - m_new).astype(BF16))
            pv = jnp.dot(vat_ref[:, pl.ds(k0, tk)], p, preferred_element_type=F32)
            acc_ref[:, qs] = alpha * acc_ref[:, qs] + pv
            m_ref[:, qs] = m_new

    nk = tq // tk
    scores(0, sa_ref)

    def pair(jj):
        j = 2 * jj
        scores(j + 1, sb_ref)
        consume(sa_ref, j)
        scores(j + 2, sa_ref)
        consume(sb_ref, j + 1)

    def quad(qq, carry):
        pair(2 * qq)
        pair(2 * qq + 1)
        return carry

    n_pairs = i * (nk // 2)
    lax.fori_loop(0, n_pairs // 2, quad, 0)

    @pl.when(n_pairs % 2 == 1)
    def _():
        pair(n_pairs - 1)

    bufs = (sa_ref, sb_ref)
    for d in range(nk):
        if d + 1 < nk:
            scores(nk * i + d + 1, bufs[(d + 1) % 2], c0=(d + 1) * tk)
        consume(bufs[d % 2], nk * i + d, c0=d * tk, causal=True)

    acc = acc_ref[...]
    att = (acc[:HEAD_DIM, :] / acc[HEAD_DIM:HEAD_DIM + 1, :]).T
    o_ref[...] = (att * z_ref[...].astype(F32)).astype(o_ref.dtype)


def _attention(qk, vt, z, ft, *, heads, tq=1024, tk=512):
    s = qk.shape[0]
    assert tq % (2 * tk) == 0 and s % tq == 0
    return pl.pallas_call(
        functools.partial(_attn_kernel, tq=tq, tk=tk),
        grid=(heads, s // tq),
        in_specs=[pl.BlockSpec((tq, HEAD_DIM), lambda h, i: (i, h)),
                  pl.BlockSpec((s, HEAD_DIM), lambda h, i: (0, heads + h)),
                  pl.BlockSpec((HEAD_DIM, s), lambda h, i: (h, 0)),
                  pl.BlockSpec((None, 1, s), lambda h, i: (h, 0, 0)),
                  pl.BlockSpec((tq, HEAD_DIM), lambda h, i: (i, h))],
        out_specs=pl.BlockSpec((tq, HEAD_DIM), lambda h, i: (i, h)),
        out_shape=jax.ShapeDtypeStruct((s, heads * HEAD_DIM), BF16),
        scratch_shapes=[pltpu.VMEM((s, LANES), BF16),
                        pltpu.VMEM((s, LANES), BF16),
                        pltpu.VMEM((VA_ROWS, s), BF16),
                        pltpu.VMEM((tq, HEAD_DIM + LANES), BF16),
                        pltpu.VMEM((tk, tq), F32),
                        pltpu.VMEM((tk, tq), F32),
                        pltpu.VMEM((1, tq), F32),
                        pltpu.VMEM((VA_ROWS, tq), F32)],
        compiler_params=_params(("parallel", "arbitrary"), 48),
        name="attn",
    )(qk, qk, vt, ft, z)


def _rnn_kernel(x_ref, z_ref, cw_ref, cb_ref, wax_ref, ba_ref, bx_ref, lam_ref, o_ref,
                xpad_ref, hc_ref, a_ref, b_ref, *, tt, wt):
    t = pl.program_id(1)
    groups = tt // SUBLANES

    @pl.when(t == 0)
    def _():
        xpad_ref[:SUBLANES, :] = jnp.zeros((SUBLANES, wt), F32)
        hc_ref[...] = jnp.zeros(hc_ref.shape, F32)

    x = x_ref[...]
    xpad_ref[SUBLANES:, :] = x
    xc = cb_ref[...] + x * cw_ref[CONV_W - 1:CONV_W, :]
    for j in range(1, CONV_W):
        xc = xc + xpad_ref[pl.ds(SUBLANES - j, tt), :] * cw_ref[CONV_W - 1 - j:CONV_W - j, :]
    xpad_ref[:SUBLANES, :] = x[tt - SUBLANES:, :]

    xcb = xc.astype(BF16)
    ga, gx = [], []
    for n in range(wt // LANES):
        g = jnp.dot(xcb[:, n * LANES:(n + 1) * LANES], wax_ref[n], preferred_element_type=F32)
        ga.append(g[:, :LANES])
        gx.append(g[:, LANES:])
    r_gate = _sigmoid(jnp.concatenate(ga, axis=1) + ba_ref[...])
    i_gate = _sigmoid(jnp.concatenate(gx, axis=1) + bx_ref[...])
    log_a = (-RG_C) * r_gate * _softplus(-lam_ref[...])
    a = jnp.exp(log_a)
    mult = jnp.sqrt(-jnp.tanh(log_a) * (a * a + 1.0))
    b = mult * (i_gate * xc)

    a3 = a.reshape(groups, SUBLANES, wt)
    b3 = b.reshape(groups, SUBLANES, wt)
    sub = lax.broadcasted_iota(jnp.int32, (groups, SUBLANES, wt), 1)
    d = 1
    while d < SUBLANES:
        keep = sub >= d
        b3 = a3 * jnp.where(keep, pltpu.roll(b3, d, 1), 0.0) + b3
        a3 = a3 * jnp.where(keep, pltpu.roll(a3, d, 1), 1.0)
        d *= 2
    a_ref[...] = a3.reshape(tt, wt)
    b_ref[...] = b3.reshape(tt, wt)

    def body(g, hprev):
        g0 = pl.multiple_of(g * SUBLANES, SUBLANES)
        hg = b_ref[pl.ds(g0, SUBLANES), :] + a_ref[pl.ds(g0, SUBLANES), :] * hprev
        b_ref[pl.ds(g0, SUBLANES), :] = hg
        return jnp.broadcast_to(hg[SUBLANES - 1:SUBLANES, :], (SUBLANES, wt))

    hlast = lax.fori_loop(0, groups, body, hc_ref[...], unroll=8)
    hc_ref[...] = hlast
    o_ref[...] = (b_ref[...] * z_ref[...].astype(F32)).astype(o_ref.dtype)


def _rnn(x_r, z_r, conv_w, conv_b, w_ax, b_a, b_x, lam, *, tt=1024, wt=512):
    s, w = x_r.shape
    nb = wt // LANES
    tile = pl.BlockSpec((tt, wt), lambda n, t: (t, n))
    vec = pl.BlockSpec((1, wt), lambda n, t: (0, n))
    return pl.pallas_call(
        functools.partial(_rnn_kernel, tt=tt, wt=wt),
        grid=(w // wt, s // tt),
        in_specs=[tile, tile,
                  pl.BlockSpec((CONV_W, wt), lambda n, t: (0, n)), vec,
                  pl.BlockSpec((nb, LANES, 2 * LANES), lambda n, t: (n, 0, 0)),
                  vec, vec, vec],
        out_specs=tile,
        out_shape=jax.ShapeDtypeStruct((s, w), BF16),
        scratch_shapes=[pltpu.VMEM((SUBLANES + tt, wt), F32),
                        pltpu.VMEM((SUBLANES, wt), F32),
                        pltpu.VMEM((tt, wt), F32),
                        pltpu.VMEM((tt, wt), F32)],
        compiler_params=_params(("parallel", "arbitrary"), 40),
        name="rnn",
    )(x_r, z_r, conv_w, conv_b.reshape(1, w), w_ax, b_a.reshape(1, w), b_x.reshape(1, w),
      lam.reshape(1, w))


def _merge_kernel(ya_ref, yr_ref, ga_ref, gr_ref, wba_ref, wbr_ref, o_ref, wa_ref, wr_ref):
    @pl.when(pl.program_id(1) == 0)
    def _():
        _cast_weight_tile(wba_ref, wa_ref)
        _cast_weight_tile(wbr_ref, wr_ref)

    y_a = jnp.dot(ya_ref[...], wa_ref[...], preferred_element_type=F32)
    acc = ga_ref[...].astype(F32) * y_a
    y_r = jnp.dot(yr_ref[...], wr_ref[...], preferred_element_type=F32)
    o_ref[...] = (acc + gr_ref[...].astype(F32) * y_r).astype(o_ref.dtype)


def _merge(ya, yr, g, w_ba, w_br, *, tm=1024, tn=512):
    s, wd = ya.shape
    d = w_ba.shape[1]
    nj = d // tn
    return pl.pallas_call(
        _merge_kernel,
        grid=(d // tn, s // tm),
        in_specs=[pl.BlockSpec((tm, wd), lambda j, i: (i, 0)),
                  pl.BlockSpec((tm, wd), lambda j, i: (i, 0)),
                  pl.BlockSpec((tm, tn), lambda j, i: (i, j)),
                  pl.BlockSpec((tm, tn), lambda j, i: (i, nj + j)),
                  pl.BlockSpec((wd, tn), lambda j, i: (0, j)),
                  pl.BlockSpec((wd, tn), lambda j, i: (0, j))],
        out_specs=pl.BlockSpec((tm, tn), lambda j, i: (i, j)),
        out_shape=jax.ShapeDtypeStruct((s, d), BF16),
        scratch_shapes=[pltpu.VMEM((wd, tn), BF16), pltpu.VMEM((wd, tn), BF16)],
        compiler_params=_params(("parallel", "arbitrary"), 56),
        name="merge",
    )(ya, yr, g, g, w_ba, w_br)


def kernel(x, c, w_ada, b_ada, norm_w, w_in, b_f, q_norm_w, k_norm_w, conv_w, conv_b, w_rg_a,
           b_rg_a, w_rg_x, b_rg_x, lru_lambda, w_br_attn, w_br_rnn, w_gate, b_gate, w_out):
    bsz, s, d = x.shape
    assert bsz == 1
    aw = w_br_attn.shape[0]
    rw = w_br_rnn.shape[0]
    heads = aw // HEAD_DIM
    x2 = x.reshape(s, d)

    mod = _ada(c, w_ada, b_ada)
    shift, scale, gate_res = mod[:, :d], mod[:, d:2 * d], mod[:, 2 * d:]
    h = _prenorm(x2, norm_w, scale, shift)

    o_f = 4 * aw
    o_xr = o_f + heads
    qk_gain = jnp.concatenate([jnp.tile(q_norm_w * (LOG2E / math.sqrt(HEAD_DIM)), heads),
                               jnp.tile(k_norm_w, heads)])
    proj = functools.partial(_proj_stream, h, w_in.T, w_transposed=True, tm=1024, tn=1024)
    qk = proj(n=2 * aw, col0=0, out_dtype=BF16, epilogue="headnorm", row_vecs=(qk_gain,),
              name="proj_qk")
    vt = proj(n=aw, col0=2 * aw, out_dtype=BF16, epilogue="plain", out_transposed=True,
              name="proj_v")
    z_a = proj(n=aw, col0=3 * aw, out_dtype=BF16, epilogue="silu", name="proj_za")
    logf = _proj(h, w_in.T, w_transposed=True, n=LANES, col0=o_f, out_dtype=F32,
                 epilogue="logsigmoid", tm=1024, tn=LANES,
                 row_vecs=(jnp.pad(b_f, (0, LANES - heads)),), name="proj_f")
    x_r = proj(n=rw, col0=o_xr, out_dtype=F32, epilogue="plain", name="proj_xr")
    z_r = proj(n=rw, col0=o_xr + rw, out_dtype=BF16, epilogue="silu", name="proj_zr")

    ft = _cumsum_t(logf)[:heads].reshape(heads, 1, s)
    ya = _attention(qk, vt, z_a, ft, heads=heads)

    w_ax = jnp.concatenate([w_rg_a, w_rg_x], axis=2).astype(BF16)
    yr = _rnn(x_r, z_r, conv_w, conv_b, w_ax, b_rg_a, b_rg_x, lru_lambda)

    g = _proj_stream(h, w_gate, n=2 * d, col0=0, out_dtype=BF16, epilogue="sigmoid_bias", tm=1024,
                     tn=1024, row_vecs=(b_gate,), name="gate")
    merged = _merge(ya, yr, g, w_br_attn, w_br_rnn)
    y = _proj_stream(merged, w_out, n=d, col0=0, out_dtype=x.dtype, epilogue="residual", tm=1024,
                     tn=1024, tiles=(x2,), row_vecs=(gate_res.reshape(-1),), nc=8, vmem_mib=60,
                     name="out")
    return y.reshape(bsz, s, d)
```

```python
import functools
import math

import jax
import jax.numpy as jnp
from jax import lax
from jax.experimental import pallas as pl
from jax.experimental.pallas import tpu as pltpu

F32 = jnp.float32
BF16 = jnp.bfloat16

LANES = 128
SUBLANES = 8
HEAD_DIM = 128
CONV_W = 4
RG_C = 8.0
EPS = 1e-6
MASK_VALUE = -1e30
LOG2E = 1.4426950408889634
MIB = 1024 * 1024


def _params(semantics, vmem_mib, flags=None):
    return pltpu.CompilerParams(dimension_semantics=semantics,
                                vmem_limit_bytes=vmem_mib * MIB, flags=flags)


def _sigmoid(x):
    return 0.5 * jnp.tanh(0.5 * x) + 0.5


def _silu(x):
    return x * _sigmoid(x)


def _softplus(x):
    return jnp.maximum(x, 0.0) + jnp.log1p(jnp.exp(-jnp.abs(x)))


def _ada_kernel(c_ref, w_ref, b_ref, o_ref, *, rows):
    d = w_ref.shape[0]

    def body(r, acc):
        r0 = pl.multiple_of(r * rows, rows)
        c = c_ref[pl.ds(r0, rows), :]
        return acc + jnp.sum(w_ref[pl.ds(r0, rows), :] * _silu(c), axis=0, keepdims=True)

    acc = lax.fori_loop(0, d // rows, body, jnp.zeros(o_ref.shape, F32))
    o_ref[...] = acc + b_ref[...]


def _ada(c, w_ada, b_ada, *, tn=512, rows=512):
    d, n = w_ada.shape
    return pl.pallas_call(
        functools.partial(_ada_kernel, rows=rows),
        grid=(n // tn,),
        in_specs=[pl.BlockSpec((d, 1), lambda j: (0, 0)),
                  pl.BlockSpec((d, tn), lambda j: (0, j)),
                  pl.BlockSpec((1, tn), lambda j: (0, j))],
        out_specs=pl.BlockSpec((1, tn), lambda j: (0, j)),
        out_shape=jax.ShapeDtypeStruct((1, n), F32),
        compiler_params=_params(("parallel",), 40),
        name="ada",
    )(c.reshape(d, 1), w_ada, b_ada.reshape(1, n))


def _prenorm_kernel(x_ref, nw_ref, scale_ref, shift_ref, o_ref):
    x = x_ref[...]
    ms = jnp.mean(x * x, axis=-1, keepdims=True)
    h = x * lax.rsqrt(ms + EPS) * nw_ref[...]
    o_ref[...] = (h * (1.0 + scale_ref[...]) + shift_ref[...]).astype(o_ref.dtype)


def _prenorm(x, norm_w, scale, shift, *, tm=512):
    s, d = x.shape
    vec = pl.BlockSpec((1, d), lambda i: (0, 0))
    return pl.pallas_call(
        _prenorm_kernel,
        grid=(s // tm,),
        in_specs=[pl.BlockSpec((tm, d), lambda i: (i, 0)), vec, vec, vec],
        out_specs=pl.BlockSpec((tm, d), lambda i: (i, 0)),
        out_shape=jax.ShapeDtypeStruct((s, d), BF16),
        compiler_params=_params(("parallel",), 40),
        name="prenorm",
    )(x, norm_w.reshape(1, d), scale, shift)


def _cast_weight_tile(w_ref, wb_ref, *, rows=128):
    n_rows = w_ref.shape[0]

    def body(r, carry):
        r0 = pl.multiple_of(r * rows, rows)
        wb_ref[pl.ds(r0, rows), :] = w_ref[pl.ds(r0, rows), :].astype(BF16)
        return carry

    lax.fori_loop(0, n_rows // rows, body, 0)


def _epilogue(y, rest, o_ref, epilogue):
    if epilogue == "plain":
        o_ref[...] = y.astype(o_ref.dtype)
    elif epilogue == "silu":
        o_ref[...] = _silu(y).astype(o_ref.dtype)
    elif epilogue == "sigmoid_bias":
        o_ref[...] = _sigmoid(y + rest[0][...]).astype(o_ref.dtype)
    elif epilogue == "headnorm":
        gain_ref = rest[0]
        for hh in range(y.shape[1] // HEAD_DIM):
            sl = slice(hh * HEAD_DIM, (hh + 1) * HEAD_DIM)
            yh = y[:, sl]
            ms = jnp.mean(yh * yh, axis=-1, keepdims=True)
            o_ref[:, sl] = (yh * lax.rsqrt(ms + EPS) * gain_ref[:, sl]).astype(o_ref.dtype)
    elif epilogue == "logsigmoid":
        z = y + rest[0][...]
        o_ref[...] = (jnp.minimum(z, 0.0) - jnp.log1p(jnp.exp(-jnp.abs(z)))).astype(o_ref.dtype)
    elif epilogue == "residual":
        x_ref, g_ref = rest[0], rest[1]
        o_ref[...] = (x_ref[...] + g_ref[...] * y).astype(o_ref.dtype)
    else:
        raise ValueError(epilogue)


def _proj_kernel(a_ref, w_ref, *rest, epilogue, w_transposed, out_transposed):
    wb_ref = rest[-1]
    o_ref = rest[-2]

    @pl.when(pl.program_id(1) == 0)
    def _():
        _cast_weight_tile(w_ref, wb_ref)

    nt_dims = (((1,), (1,)), ((), ()))
    if out_transposed:
        y = lax.dot_general(wb_ref[...], a_ref[...], nt_dims, preferred_element_type=F32)
    elif w_transposed:
        y = lax.dot_general(a_ref[...], wb_ref[...], nt_dims, preferred_element_type=F32)
    else:
        y = jnp.dot(a_ref[...], wb_ref[...], preferred_element_type=F32)
    _epilogue(y, rest[:-2], o_ref, epilogue)


def _proj(a, w, *, n, col0, out_dtype, epilogue, tm, tn, w_transposed=False, out_transposed=False,
          row_vecs=(), tiles=(), name):
    m, k = a.shape
    assert not out_transposed or (w_transposed and not row_vecs and not tiles)
    assert n % tn == 0 and m % tm == 0
    if w_transposed:
        assert col0 % SUBLANES == 0
        w_spec = pl.BlockSpec((pl.Element(tn), pl.Element(k)),
                              lambda j, i: (pl.multiple_of(col0 + j * tn, SUBLANES), 0))
        wb_shape = (tn, k)
    else:
        assert col0 % tn == 0
        w_spec = pl.BlockSpec((k, tn), lambda j, i: (0, col0 // tn + j))
        wb_shape = (k, tn)
    in_specs = [pl.BlockSpec((tm, k), lambda j, i: (i, 0)), w_spec]
    args = [a, w]
    for t in tiles:
        in_specs.append(pl.BlockSpec((tm, tn), lambda j, i: (i, j)))
        args.append(t)
    for v in row_vecs:
        in_specs.append(pl.BlockSpec((1, tn), lambda j, i: (0, j)))
        args.append(v.reshape(1, n))
    if out_transposed:
        out_spec = pl.BlockSpec((tn, tm), lambda j, i: (j, i))
        out_shape = jax.ShapeDtypeStruct((n, m), out_dtype)
    else:
        out_spec = pl.BlockSpec((tm, tn), lambda j, i: (i, j))
        out_shape = jax.ShapeDtypeStruct((m, n), out_dtype)
    return pl.pallas_call(
        functools.partial(_proj_kernel, epilogue=epilogue, w_transposed=w_transposed,
                          out_transposed=out_transposed),
        grid=(n // tn, m // tm),
        in_specs=in_specs,
        out_specs=out_spec,
        out_shape=out_shape,
        scratch_shapes=[pltpu.VMEM(wb_shape, BF16)],
        compiler_params=_params(("parallel", "arbitrary"), 56),
        name=name,
    )(*args)


W_CHUNKS = 4


def _stream_step(s, *, ni, nj, nc):
    u = jnp.maximum(s - nc, 0)
    j, i = u // ni, u % ni
    pro = s < nc
    last = j == nj - 1
    tile = jnp.where(pro, 0, jnp.minimum(j + 1, nj - 1))
    chunk = jnp.where(pro, s, jnp.where(last, nc - 1, jnp.minimum(i, nc - 1)))
    return tile, chunk, j, i


def _proj_stream_kernel(a_ref, wc_ref, *rest, epilogue, w_transposed, out_transposed, ni, nc):
    wb_ref = rest[-1]
    o_ref = rest[-2]
    s = pl.program_id(0)
    rows = wc_ref.shape[0]
    u = jnp.maximum(s - nc, 0)
    j = u // ni
    chunk = jnp.minimum(u % ni, nc - 1)

    def cast_chunk(slot, c):
        r0 = pl.multiple_of(c * rows, rows)
        wb_ref[slot, pl.ds(r0, rows), :] = wc_ref[...].astype(BF16)

    @pl.when(s < nc)
    def _():
        cast_chunk(0, s)

    @pl.when(s >= nc)
    def _():
        wb = wb_ref[j % 2]
        nt_dims = (((1,), (1,)), ((), ()))
        if out_transposed:
            y = lax.dot_general(wb, a_ref[...], nt_dims, preferred_element_type=F32)
        elif w_transposed:
            y = lax.dot_general(a_ref[...], wb, nt_dims, preferred_element_type=F32)
        else:
            y = jnp.dot(a_ref[...], wb, preferred_element_type=F32)
        _epilogue(y, rest[:-2], o_ref, epilogue)
        cast_chunk((j + 1) % 2, chunk)


def _proj_stream(a, w, *, n, col0, out_dtype, epilogue, tm, tn, w_transposed=False,
                 out_transposed=False, row_vecs=(), tiles=(), nc=W_CHUNKS, vmem_mib=56, name):
    m, k = a.shape
    nj, ni = n // tn, m // tm
    assert n % tn == 0 and m % tm == 0 and ni >= nc
    assert not out_transposed or (w_transposed and not row_vecs and not tiles)

    where_to = functools.partial(_stream_step, ni=ni, nj=nj, nc=nc)

    if w_transposed:
        assert col0 % SUBLANES == 0 and tn % nc == 0
        cr = tn // nc

        def w_index(s):
            tile, chunk, _, _ = where_to(s)
            return pl.multiple_of(col0 + tile * tn + chunk * cr, SUBLANES), 0

        w_spec = pl.BlockSpec((pl.Element(cr), pl.Element(k)), w_index)
        wb_shape = (2, tn, k)
    else:
        assert col0 % tn == 0 and k % nc == 0
        ck = k // nc

        def w_index(s):
            tile, chunk, _, _ = where_to(s)
            return chunk, col0 // tn + tile

        w_spec = pl.BlockSpec((ck, tn), w_index)
        wb_shape = (2, k, tn)

    def a_index(s):
        _, _, _, i = where_to(s)
        return i, 0

    def vec_index(s):
        _, _, j, _ = where_to(s)
        return 0, j

    def out_index(s):
        _, _, j, i = where_to(s)
        return (j, i) if out_transposed else (i, j)

    in_specs = [pl.BlockSpec((tm, k), a_index), w_spec]
    args = [a, w]
    for t in tiles:
        in_specs.append(pl.BlockSpec((tm, tn), out_index))
        args.append(t)
    for v in row_vecs:
        in_specs.append(pl.BlockSpec((1, tn), vec_index))
        args.append(v.reshape(1, n))
    if out_transposed:
        out_spec = pl.BlockSpec((tn, tm), out_index)
        out_shape = jax.ShapeDtypeStruct((n, m), out_dtype)
    else:
        out_spec = pl.BlockSpec((tm, tn), out_index)
        out_shape = jax.ShapeDtypeStruct((m, n), out_dtype)
    return pl.pallas_call(
        functools.partial(_proj_stream_kernel, epilogue=epilogue, w_transposed=w_transposed,
                          out_transposed=out_transposed, ni=ni, nc=nc),
        grid=(nc + nj * ni,),
        in_specs=in_specs,
        out_specs=out_spec,
        out_shape=out_shape,
        scratch_shapes=[pltpu.VMEM(wb_shape, BF16)],
        compiler_params=_params(("arbitrary",), vmem_mib),
        name=name,
    )(*args)


def _cumsum_kernel(x_ref, ft_ref, *, chunk):
    s = x_ref.shape[0]
    row = lax.broadcasted_iota(jnp.int32, (chunk, LANES), 0)

    def body(c, carry):
        s0 = pl.multiple_of(c * chunk, chunk)
        x = x_ref[pl.ds(s0, chunk), :]
        d = 1
        while d < chunk:
            x = x + jnp.where(row >= d, pltpu.roll(x, d, 0), 0.0)
            d *= 2
        x = x + carry
        ft_ref[:, pl.ds(s0, chunk)] = x.T
        return x[chunk - 1:chunk, :]

    lax.fori_loop(0, s // chunk, body, jnp.zeros((1, LANES), F32))


def _cumsum_t(logf, *, chunk=512):
    s = logf.shape[0]
    return pl.pallas_call(
        functools.partial(_cumsum_kernel, chunk=chunk),
        out_shape=jax.ShapeDtypeStruct((LANES, s), F32),
        compiler_params=_params(None, 40),
        name="cumsum",
    )(logf)


N_SPLIT = 3


def _decay_features(ft_ref, qf_ref, kf_ref, *, chunk=512):
    s = ft_ref.shape[1]
    row = lax.broadcasted_iota(jnp.int32, (LANES, chunk), 0)

    def body(c, carry):
        c0 = pl.multiple_of(c * chunk, chunk)
        rem = ft_ref[:, pl.ds(c0, chunk)] * LOG2E
        qf = jnp.where((row >= N_SPLIT) & (row < 2 * N_SPLIT), 1.0, 0.0)
        kf = jnp.where(row < N_SPLIT, 1.0, 0.0)
        for p in range(N_SPLIT):
            piece = rem.astype(BF16).astype(F32)
            rem = rem - piece
            qf = jnp.where(row == p, piece, qf)
            kf = jnp.where(row == N_SPLIT + p, -piece, kf)
        qf_ref[pl.ds(c0, chunk), :] = qf.T.astype(BF16)
        kf_ref[pl.ds(c0, chunk), :] = kf.T.astype(BF16)
        return carry

    lax.fori_loop(0, s // chunk, body, 0)


VA_ROWS = HEAD_DIM + 16
QUERY_CHUNK = 256


def _attn_kernel(q_ref, k_ref, vt_ref, ft_ref, z_ref, o_ref, qf_ref, kf_ref, vat_ref, qa_ref, sa_ref,
                 sb_ref, m_ref, acc_ref, *, tq, tk):
    i = pl.program_id(1)
    nt_dims = (((1,), (1,)), ((), ()))

    @pl.when(i == 0)
    def _():
        _decay_features(ft_ref, qf_ref, kf_ref)
        vat_ref[:HEAD_DIM, :] = vt_ref[...]
        row = lax.broadcasted_iota(jnp.int32, (VA_ROWS - HEAD_DIM, vat_ref.shape[1]), 0)
        vat_ref[HEAD_DIM:, :] = jnp.where(row == 0, 1.0, 0.0).astype(BF16)

    q0 = pl.multiple_of(i * tq, tq)
    qa_ref[:, :HEAD_DIM] = q_ref[...]
    qa_ref[:, HEAD_DIM:] = qf_ref[pl.ds(q0, tq), :]

    m_ref[...] = jnp.full(m_ref.shape, MASK_VALUE, F32)
    acc_ref[...] = jnp.zeros(acc_ref.shape, F32)

    def scores(j, dst_ref, c0=0):
        k0 = pl.multiple_of(j * tk, tk)
        ka = jnp.concatenate([k_ref[pl.ds(k0, tk), :], kf_ref[pl.ds(k0, tk), :]], axis=1)
        dst_ref[:, c0:] = lax.dot_general(ka, qa_ref[c0:, :], nt_dims,
                                          preferred_element_type=F32)

    def consume(src_ref, j, c0=0, causal=False):
        k0 = pl.multiple_of(j * tk, tk)
        row0 = pl.multiple_of(jnp.minimum(i, 0) * SUBLANES, SUBLANES)
        for q1 in range(c0, tq, QUERY_CHUNK):
            qs = slice(q1, q1 + QUERY_CHUNK)
            s2 = src_ref[pl.ds(row0, tk), qs]
            if causal and q1 - c0 < tk:
                keys = lax.broadcasted_iota(jnp.int32, (tk, QUERY_CHUNK), 0)
                qrys = lax.broadcasted_iota(jnp.int32, (tk, QUERY_CHUNK), 1) + (q1 - c0)
                s2 = jnp.where(qrys >= keys, s2, MASK_VALUE)
            m_prev = m_ref[:, qs]
            m_new = jnp.maximum(m_prev, jnp.max(s2, axis=0, keepdims=True))
            alpha = jnp.exp2(m_prev - m_new)
            p = jnp.exp2((s2 - m_new).astype(BF16))
            pv = jnp.dot(vat_ref[:, pl.ds(k0, tk)], p, preferred_element_type=F32)
            acc_ref[:, qs] = alpha * acc_ref[:, qs] + pv
            m_ref[:, qs] = m_new

    nk = tq // tk
    scores(0, sa_ref)

    def pair(jj):
        j = 2 * jj
        scores(j + 1, sb_ref)
        consume(sa_ref, j)
        scores(j + 2, sa_ref)
        consume(sb_ref, j + 1)

    def quad(qq, carry):
        pair(2 * qq)
        pair(2 * qq + 1)
        return carry

    n_pairs = i * (nk // 2)
    lax.fori_loop(0, n_pairs // 2, quad, 0)

    @pl.when(n_pairs % 2 == 1)
    def _():
        pair(n_pairs - 1)

    bufs = (sa_ref, sb_ref)
    for d in range(nk):
        if d + 1 < nk:
            scores(nk * i + d + 1, bufs[(d + 1) % 2], c0=(d + 1) * tk)
        consume(bufs[d % 2], nk * i + d, c0=d * tk, causal=True)

    acc = acc_ref[...]
    att = (acc[:HEAD_DIM, :] / acc[HEAD_DIM:HEAD_DIM + 1, :]).T
    o_ref[...] = (att * z_ref[...].astype(F32)).astype(o_ref.dtype)


def _attention(qk, vt, z, ft, *, heads, tq=1024, tk=512):
    s = qk.shape[0]
    assert tq % (2 * tk) == 0 and s % tq == 0
    return pl.pallas_call(
        functools.partial(_attn_kernel, tq=tq, tk=tk),
        grid=(heads, s // tq),
        in_specs=[pl.BlockSpec((tq, HEAD_DIM), lambda h, i: (i, h)),
                  pl.BlockSpec((s, HEAD_DIM), lambda h, i: (0, heads + h)),
                  pl.BlockSpec((HEAD_DIM, s), lambda h, i: (h, 0)),
                  pl.BlockSpec((None, 1, s), lambda h, i: (h, 0, 0)),
                  pl.BlockSpec((tq, HEAD_DIM), lambda h, i: (i, h))],
        out_specs=pl.BlockSpec((tq, HEAD_DIM), lambda h, i: (i, h)),
        out_shape=jax.ShapeDtypeStruct((s, heads * HEAD_DIM), BF16),
        scratch_shapes=[pltpu.VMEM((s, LANES), BF16),
                        pltpu.VMEM((s, LANES), BF16),
                        pltpu.VMEM((VA_ROWS, s), BF16),
                        pltpu.VMEM((tq, HEAD_DIM + LANES), BF16),
                        pltpu.VMEM((tk, tq), F32),
                        pltpu.VMEM((tk, tq), F32),
                        pltpu.VMEM((1, tq), F32),
                        pltpu.VMEM((VA_ROWS, tq), F32)],
        compiler_params=_params(("parallel", "arbitrary"), 48),
        name="attn",
    )(qk, qk, vt, ft, z)


def _rnn_kernel(x_ref, z_ref, cw_ref, cb_ref, wax_ref, ba_ref, bx_ref, lam_ref, o_ref,
                xpad_ref, hc_ref, a_ref, b_ref, *, tt, wt):
    t = pl.program_id(1)
    groups = tt // SUBLANES

    @pl.when(t == 0)
    def _():
        xpad_ref[:SUBLANES, :] = jnp.zeros((SUBLANES, wt), F32)
        hc_ref[...] = jnp.zeros(hc_ref.shape, F32)

    x = x_ref[...]
    xpad_ref[SUBLANES:, :] = x
    xc = cb_ref[...] + x * cw_ref[CONV_W - 1:CONV_W, :]
    for j in range(1, CONV_W):
        xc = xc + xpad_ref[pl.ds(SUBLANES - j, tt), :] * cw_ref[CONV_W - 1 - j:CONV_W - j, :]
    xpad_ref[:SUBLANES, :] = x[tt - SUBLANES:, :]

    xcb = xc.astype(BF16)
    ga, gx = [], []
    for n in range(wt // LANES):
        g = jnp.dot(xcb[:, n * LANES:(n + 1) * LANES], wax_ref[n], preferred_element_type=F32)
        ga.append(g[:, :LANES])
        gx.append(g[:, LANES:])
    r_gate = _sigmoid(jnp.concatenate(ga, axis=1) + ba_ref[...])
    i_gate = _sigmoid(jnp.concatenate(gx, axis=1) + bx_ref[...])
    log_a = (-RG_C) * r_gate * _softplus(-lam_ref[...])
    a = jnp.exp(log_a)
    mult = jnp.sqrt(-jnp.tanh(log_a) * (a * a + 1.0))
    b = mult * (i_gate * xc)

    a3 = a.reshape(groups, SUBLANES, wt)
    b3 = b.reshape(groups, SUBLANES, wt)
    sub = lax.broadcasted_iota(jnp.int32, (groups, SUBLANES, wt), 1)
    d = 1
    while d < SUBLANES:
        keep = sub >= d
        b3 = a3 * jnp.where(keep, pltpu.roll(b3, d, 1), 0.0) + b3
        a3 = a3 * jnp.where(keep, pltpu.roll(a3, d, 1), 1.0)
        d *= 2
    a_ref[...] = a3.reshape(tt, wt)
    b_ref[...] = b3.reshape(tt, wt)

    def body(g, hprev):
        g0 = pl.multiple_of(g * SUBLANES, SUBLANES)
        hg = b_ref[pl.ds(g0, SUBLANES), :] + a_ref[pl.ds(g0, SUBLANES), :] * hprev
        b_ref[pl.ds(g0, SUBLANES), :] = hg
        return jnp.broadcast_to(hg[SUBLANES - 1:SUBLANES, :], (SUBLANES, wt))

    hlast = lax.fori_loop(0, groups, body, hc_ref[...], unroll=8)
    hc_ref[...] = hlast
    o_ref[...] = (b_ref[...] * z_ref[...].astype(F32)).astype(o_ref.dtype)


def _rnn(x_r, z_r, conv_w, conv_b, w_ax, b_a, b_x, lam, *, tt=1024, wt=512):
    s, w = x_r.shape
    nb = wt // LANES
    tile = pl.BlockSpec((tt, wt), lambda n, t: (t, n))
    vec = pl.BlockSpec((1, wt), lambda n, t: (0, n))
    return pl.pallas_call(
        functools.partial(_rnn_kernel, tt=tt, wt=wt),
        grid=(w // wt, s // tt),
        in_specs=[tile, tile,
                  pl.BlockSpec((CONV_W, wt), lambda n, t: (0, n)), vec,
                  pl.BlockSpec((nb, LANES, 2 * LANES), lambda n, t: (n, 0, 0)),
                  vec, vec, vec],
        out_specs=tile,
        out_shape=jax.ShapeDtypeStruct((s, w), BF16),
        scratch_shapes=[pltpu.VMEM((SUBLANES + tt, wt), F32),
                        pltpu.VMEM((SUBLANES, wt), F32),
                        pltpu.VMEM((tt, wt), F32),
                        pltpu.VMEM((tt, wt), F32)],
        compiler_params=_params(("parallel", "arbitrary"), 40),
        name="rnn",
    )(x_r, z_r, conv_w, conv_b.reshape(1, w), w_ax, b_a.reshape(1, w), b_x.reshape(1, w),
      lam.reshape(1, w))


def _merge_kernel(ya_ref, yr_ref, ga_ref, gr_ref, wca_ref, wcr_ref, o_ref, wa_ref, wr_ref, *, ni, nc):
    s = pl.program_id(0)
    rows = wca_ref.shape[0]
    u = jnp.maximum(s - nc, 0)
    j = u // ni
    chunk = jnp.minimum(u % ni, nc - 1)

    def cast_chunk(slot, c):
        r0 = pl.multiple_of(c * rows, rows)
        wa_ref[slot, pl.ds(r0, rows), :] = wca_ref[...].astype(BF16)
        wr_ref[slot, pl.ds(r0, rows), :] = wcr_ref[...].astype(BF16)

    @pl.when(s < nc)
    def _():
        cast_chunk(0, s)

    @pl.when(s >= nc)
    def _():
        y_a = jnp.dot(ya_ref[...], wa_ref[j % 2], preferred_element_type=F32)
        acc = ga_ref[...].astype(F32) * y_a
        y_r = jnp.dot(yr_ref[...], wr_ref[j % 2], preferred_element_type=F32)
        o_ref[...] = (acc + gr_ref[...].astype(F32) * y_r).astype(o_ref.dtype)
        cast_chunk((j + 1) % 2, chunk)


def _merge(ya, yr, g, w_ba, w_br, *, tm=1024, tn=1024, nc=8):
    s, wd = ya.shape
    d = w_ba.shape[1]
    nj, ni = d // tn, s // tm
    assert d % tn == 0 and s % tm == 0 and wd % nc == 0 and ni >= nc
    step = functools.partial(_stream_step, ni=ni, nj=nj, nc=nc)
    act = pl.BlockSpec((tm, wd), lambda t: (step(t)[3], 0))
    wchunk = pl.BlockSpec((wd // nc, tn), lambda t: (step(t)[1], step(t)[0]))
    return pl.pallas_call(
        functools.partial(_merge_kernel, ni=ni, nc=nc),
        grid=(nc + nj * ni,),
        in_specs=[act, act,
                  pl.BlockSpec((tm, tn), lambda t: (step(t)[3], step(t)[2])),
                  pl.BlockSpec((tm, tn), lambda t: (step(t)[3], nj + step(t)[2])),
                  wchunk, wchunk],
        out_specs=pl.BlockSpec((tm, tn), lambda t: (step(t)[3], step(t)[2])),
        out_shape=jax.ShapeDtypeStruct((s, d), BF16),
        scratch_shapes=[pltpu.VMEM((2, wd, tn), BF16), pltpu.VMEM((2, wd, tn), BF16)],
        compiler_params=_params(("arbitrary",), 60),
        name="merge",
    )(ya, yr, g, g, w_ba, w_br)


def kernel(x, c, w_ada, b_ada, norm_w, w_in, b_f, q_norm_w, k_norm_w, conv_w, conv_b, w_rg_a,
           b_rg_a, w_rg_x, b_rg_x, lru_lambda, w_br_attn, w_br_rnn, w_gate, b_gate, w_out):
    bsz, s, d = x.shape
    assert bsz == 1
    aw = w_br_attn.shape[0]
    rw = w_br_rnn.shape[0]
    heads = aw // HEAD_DIM
    x2 = x.reshape(s, d)

    mod = _ada(c, w_ada, b_ada)
    shift, scale, gate_res = mod[:, :d], mod[:, d:2 * d], mod[:, 2 * d:]
    h = _prenorm(x2, norm_w, scale, shift)

    o_f = 4 * aw
    o_xr = o_f + heads
    qk_gain = jnp.concatenate([jnp.tile(q_norm_w * (LOG2E / math.sqrt(HEAD_DIM)), heads),
                               jnp.tile(k_norm_w, heads)])
    proj = functools.partial(_proj_stream, h, w_in.T, w_transposed=True, tm=1024, tn=1024)
    qk = proj(n=2 * aw, col0=0, out_dtype=BF16, epilogue="headnorm", row_vecs=(qk_gain,),
              name="proj_qk")
    vt = proj(n=aw, col0=2 * aw, out_dtype=BF16, epilogue="plain", out_transposed=True,
              name="proj_v")
    z_a = proj(n=aw, col0=3 * aw, out_dtype=BF16, epilogue="silu", name="proj_za")
    logf = _proj(h, w_in.T, w_transposed=True, n=LANES, col0=o_f, out_dtype=F32,
                 epilogue="logsigmoid", tm=1024, tn=LANES,
                 row_vecs=(jnp.pad(b_f, (0, LANES - heads)),), name="proj_f")
    x_r = proj(n=rw, col0=o_xr, out_dtype=F32, epilogue="plain", name="proj_xr")
    z_r = proj(n=rw, col0=o_xr + rw, out_dtype=BF16, epilogue="silu", name="proj_zr")

    ft = _cumsum_t(logf)[:heads].reshape(heads, 1, s)
    ya = _attention(qk, vt, z_a, ft, heads=heads)

    w_ax = jnp.concatenate([w_rg_a, w_rg_x], axis=2).astype(BF16)
    yr = _rnn(x_r, z_r, conv_w, conv_b, w_ax, b_rg_a, b_rg_x, lru_lambda)

    g = _proj_stream(h, w_gate, n=2 * d, col0=0, out_dtype=BF16, epilogue="sigmoid_bias", tm=1024,
                     tn=1024, row_vecs=(b_gate,), name="gate")
    merged = _merge(ya, yr, g, w_br_attn, w_br_rnn)
    y = _proj_stream(merged, w_out, n=d, col0=0, out_dtype=x.dtype, epilogue="residual", tm=1024,
                     tn=1024, tiles=(x2,), row_vecs=(gate_res.reshape(-1),), nc=8, vmem_mib=60,
                     name="out")
    return y.reshape(bsz, s, d)
```

```python
import functools
import math

import jax
import jax.numpy as jnp
from jax import lax
from jax.experimental import pallas as pl
from jax.experimental.pallas import tpu as pltpu

F32 = jnp.float32
BF16 = jnp.bfloat16

LANES = 128
SUBLANES = 8
HEAD_DIM = 128
CONV_W = 4
RG_C = 8.0
EPS = 1e-6
MASK_VALUE = -1e30
LOG2E = 1.4426950408889634
MIB = 1024 * 1024


def _params(semantics, vmem_mib, flags=None):
    return pltpu.CompilerParams(dimension_semantics=semantics,
                                vmem_limit_bytes=vmem_mib * MIB, flags=flags)


def _sigmoid(x):
    return 0.5 * jnp.tanh(0.5 * x) + 0.5


def _silu(x):
    return x * _sigmoid(x)


def _softplus(x):
    return jnp.maximum(x, 0.0) + jnp.log1p(jnp.exp(-jnp.abs(x)))


def _ada_kernel(c_ref, w_ref, b_ref, o_ref, *, rows):
    d = w_ref.shape[0]

    def body(r, acc):
        r0 = pl.multiple_of(r * rows, rows)
        c = c_ref[pl.ds(r0, rows), :]
        return acc + jnp.sum(w_ref[pl.ds(r0, rows), :] * _silu(c), axis=0, keepdims=True)

    acc = lax.fori_loop(0, d // rows, body, jnp.zeros(o_ref.shape, F32))
    o_ref[...] = acc + b_ref[...]


def _ada(c, w_ada, b_ada, *, tn=512, rows=512):
    d, n = w_ada.shape
    return pl.pallas_call(
        functools.partial(_ada_kernel, rows=rows),
        grid=(n // tn,),
        in_specs=[pl.BlockSpec((d, 1), lambda j: (0, 0)),
                  pl.BlockSpec((d, tn), lambda j: (0, j)),
                  pl.BlockSpec((1, tn), lambda j: (0, j))],
        out_specs=pl.BlockSpec((1, tn), lambda j: (0, j)),
        out_shape=jax.ShapeDtypeStruct((1, n), F32),
        compiler_params=_params(("parallel",), 40),
        name="ada",
    )(c.reshape(d, 1), w_ada, b_ada.reshape(1, n))


def _prenorm_kernel(x_ref, nw_ref, scale_ref, shift_ref, o_ref):
    x = x_ref[...]
    ms = jnp.mean(x * x, axis=-1, keepdims=True)
    h = x * lax.rsqrt(ms + EPS) * nw_ref[...]
    o_ref[...] = (h * (1.0 + scale_ref[...]) + shift_ref[...]).astype(o_ref.dtype)


def _prenorm(x, norm_w, scale, shift, *, tm=512):
    s, d = x.shape
    vec = pl.BlockSpec((1, d), lambda i: (0, 0))
    return pl.pallas_call(
        _prenorm_kernel,
        grid=(s // tm,),
        in_specs=[pl.BlockSpec((tm, d), lambda i: (i, 0)), vec, vec, vec],
        out_specs=pl.BlockSpec((tm, d), lambda i: (i, 0)),
        out_shape=jax.ShapeDtypeStruct((s, d), BF16),
        compiler_params=_params(("parallel",), 40),
        name="prenorm",
    )(x, norm_w.reshape(1, d), scale, shift)


def _cast_weight_tile(w_ref, wb_ref, *, rows=128):
    n_rows = w_ref.shape[0]

    def body(r, carry):
        r0 = pl.multiple_of(r * rows, rows)
        wb_ref[pl.ds(r0, rows), :] = w_ref[pl.ds(r0, rows), :].astype(BF16)
        return carry

    lax.fori_loop(0, n_rows // rows, body, 0)


def _epilogue(y, rest, o_ref, epilogue):
    if epilogue == "plain":
        o_ref[...] = y.astype(o_ref.dtype)
    elif epilogue == "silu":
        o_ref[...] = _silu(y).astype(o_ref.dtype)
    elif epilogue == "sigmoid_bias":
        o_ref[...] = _sigmoid(y + rest[0][...]).astype(o_ref.dtype)
    elif epilogue == "headnorm":
        gain_ref = rest[0]
        for hh in range(y.shape[1] // HEAD_DIM):
            sl = slice(hh * HEAD_DIM, (hh + 1) * HEAD_DIM)
            yh = y[:, sl]
            ms = jnp.mean(yh * yh, axis=-1, keepdims=True)
            o_ref[:, sl] = (yh * lax.rsqrt(ms + EPS) * gain_ref[:, sl]).astype(o_ref.dtype)
    elif epilogue == "logsigmoid":
        z = y + rest[0][...]
        o_ref[...] = (jnp.minimum(z, 0.0) - jnp.log1p(jnp.exp(-jnp.abs(z)))).astype(o_ref.dtype)
    elif epilogue == "residual":
        x_ref, g_ref = rest[0], rest[1]
        o_ref[...] = (x_ref[...] + g_ref[...] * y).astype(o_ref.dtype)
    else:
        raise ValueError(epilogue)


def _proj_kernel(a_ref, w_ref, *rest, epilogue, w_transposed, out_transposed):
    wb_ref = rest[-1]
    o_ref = rest[-2]

    @pl.when(pl.program_id(1) == 0)
    def _():
        _cast_weight_tile(w_ref, wb_ref)

    nt_dims = (((1,), (1,)), ((), ()))
    if out_transposed:
        y = lax.dot_general(wb_ref[...], a_ref[...], nt_dims, preferred_element_type=F32)
    elif w_transposed:
        y = lax.dot_general(a_ref[...], wb_ref[...], nt_dims, preferred_element_type=F32)
    else:
        y = jnp.dot(a_ref[...], wb_ref[...], preferred_element_type=F32)
    _epilogue(y, rest[:-2], o_ref, epilogue)


def _proj(a, w, *, n, col0, out_dtype, epilogue, tm, tn, w_transposed=False, out_transposed=False,
          row_vecs=(), tiles=(), name):
    m, k = a.shape
    assert not out_transposed or (w_transposed and not row_vecs and not tiles)
    assert n % tn == 0 and m % tm == 0
    if w_transposed:
        assert col0 % SUBLANES == 0
        w_spec = pl.BlockSpec((pl.Element(tn), pl.Element(k)),
                              lambda j, i: (pl.multiple_of(col0 + j * tn, SUBLANES), 0))
        wb_shape = (tn, k)
    else:
        assert col0 % tn == 0
        w_spec = pl.BlockSpec((k, tn), lambda j, i: (0, col0 // tn + j))
        wb_shape = (k, tn)
    in_specs = [pl.BlockSpec((tm, k), lambda j, i: (i, 0)), w_spec]
    args = [a, w]
    for t in tiles:
        in_specs.append(pl.BlockSpec((tm, tn), lambda j, i: (i, j)))
        args.append(t)
    for v in row_vecs:
        in_specs.append(pl.BlockSpec((1, tn), lambda j, i: (0, j)))
        args.append(v.reshape(1, n))
    if out_transposed:
        out_spec = pl.BlockSpec((tn, tm), lambda j, i: (j, i))
        out_shape = jax.ShapeDtypeStruct((n, m), out_dtype)
    else:
        out_spec = pl.BlockSpec((tm, tn), lambda j, i: (i, j))
        out_shape = jax.ShapeDtypeStruct((m, n), out_dtype)
    return pl.pallas_call(
        functools.partial(_proj_kernel, epilogue=epilogue, w_transposed=w_transposed,
                          out_transposed=out_transposed),
        grid=(n // tn, m // tm),
        in_specs=in_specs,
        out_specs=out_spec,
        out_shape=out_shape,
        scratch_shapes=[pltpu.VMEM(wb_shape, BF16)],
        compiler_params=_params(("parallel", "arbitrary"), 56),
        name=name,
    )(*args)


W_CHUNKS = 4


def _stream_step(s, *, ni, nj, nc):
    u = jnp.maximum(s - nc, 0)
    j, i = u // ni, u % ni
    pro = s < nc
    last = j == nj - 1
    tile = jnp.where(pro, 0, jnp.minimum(j + 1, nj - 1))
    chunk = jnp.where(pro, s, jnp.where(last, nc - 1, jnp.minimum(i, nc - 1)))
    return tile, chunk, j, i


def _proj_stream_kernel(a_ref, wc_ref, *rest, epilogue, w_transposed, out_transposed, ni, nc):
    wb_ref = rest[-1]
    o_ref = rest[-2]
    s = pl.program_id(0)
    rows = wc_ref.shape[0]
    u = jnp.maximum(s - nc, 0)
    j = u // ni
    chunk = jnp.minimum(u % ni, nc - 1)

    def cast_chunk(slot, c):
        r0 = pl.multiple_of(c * rows, rows)
        wb_ref[slot, pl.ds(r0, rows), :] = wc_ref[...].astype(BF16)

    @pl.when(s < nc)
    def _():
        cast_chunk(0, s)

    @pl.when(s >= nc)
    def _():
        wb = wb_ref[j % 2]
        nt_dims = (((1,), (1,)), ((), ()))
        if out_transposed:
            y = lax.dot_general(wb, a_ref[...], nt_dims, preferred_element_type=F32)
        elif w_transposed:
            y = lax.dot_general(a_ref[...], wb, nt_dims, preferred_element_type=F32)
        else:
            y = jnp.dot(a_ref[...], wb, preferred_element_type=F32)
        _epilogue(y, rest[:-2], o_ref, epilogue)
        cast_chunk((j + 1) % 2, chunk)


def _proj_stream(a, w, *, n, col0, out_dtype, epilogue, tm, tn, w_transposed=False,
                 out_transposed=False, row_vecs=(), tiles=(), nc=W_CHUNKS, vmem_mib=56, name):
    m, k = a.shape
    nj, ni = n // tn, m // tm
    assert n % tn == 0 and m % tm == 0 and ni >= nc
    assert not out_transposed or (w_transposed and not row_vecs and not tiles)

    where_to = functools.partial(_stream_step, ni=ni, nj=nj, nc=nc)

    if w_transposed:
        assert col0 % SUBLANES == 0 and tn % nc == 0
        cr = tn // nc

        def w_index(s):
            tile, chunk, _, _ = where_to(s)
            return pl.multiple_of(col0 + tile * tn + chunk * cr, SUBLANES), 0

        w_spec = pl.BlockSpec((pl.Element(cr), pl.Element(k)), w_index)
        wb_shape = (2, tn, k)
    else:
        assert col0 % tn == 0 and k % nc == 0
        ck = k // nc

        def w_index(s):
            tile, chunk, _, _ = where_to(s)
            return chunk, col0 // tn + tile

        w_spec = pl.BlockSpec((ck, tn), w_index)
        wb_shape = (2, k, tn)

    def a_index(s):
        _, _, _, i = where_to(s)
        return i, 0

    def vec_index(s):
        _, _, j, _ = where_to(s)
        return 0, j

    def out_index(s):
        _, _, j, i = where_to(s)
        return (j, i) if out_transposed else (i, j)

    in_specs = [pl.BlockSpec((tm, k), a_index), w_spec]
    args = [a, w]
    for t in tiles:
        in_specs.append(pl.BlockSpec((tm, tn), out_index))
        args.append(t)
    for v in row_vecs:
        in_specs.append(pl.BlockSpec((1, tn), vec_index))
        args.append(v.reshape(1, n))
    if out_transposed:
        out_spec = pl.BlockSpec((tn, tm), out_index)
        out_shape = jax.ShapeDtypeStruct((n, m), out_dtype)
    else:
        out_spec = pl.BlockSpec((tm, tn), out_index)
        out_shape = jax.ShapeDtypeStruct((m, n), out_dtype)
    return pl.pallas_call(
        functools.partial(_proj_stream_kernel, epilogue=epilogue, w_transposed=w_transposed,
                          out_transposed=out_transposed, ni=ni, nc=nc),
        grid=(nc + nj * ni,),
        in_specs=in_specs,
        out_specs=out_spec,
        out_shape=out_shape,
        scratch_shapes=[pltpu.VMEM(wb_shape, BF16)],
        compiler_params=_params(("arbitrary",), vmem_mib),
        name=name,
    )(*args)


def _cumsum_kernel(x_ref, ft_ref, *, chunk):
    s = x_ref.shape[0]
    row = lax.broadcasted_iota(jnp.int32, (chunk, LANES), 0)

    def body(c, carry):
        s0 = pl.multiple_of(c * chunk, chunk)
        x = x_ref[pl.ds(s0, chunk), :]
        d = 1
        while d < chunk:
            x = x + jnp.where(row >= d, pltpu.roll(x, d, 0), 0.0)
            d *= 2
        x = x + carry
        ft_ref[:, pl.ds(s0, chunk)] = x.T
        return x[chunk - 1:chunk, :]

    lax.fori_loop(0, s // chunk, body, jnp.zeros((1, LANES), F32))


def _cumsum_t(logf, *, chunk=512):
    s = logf.shape[0]
    return pl.pallas_call(
        functools.partial(_cumsum_kernel, chunk=chunk),
        out_shape=jax.ShapeDtypeStruct((LANES, s), F32),
        compiler_params=_params(None, 40),
        name="cumsum",
    )(logf)


N_SPLIT = 3


def _decay_features(ft_ref, qf_ref, kf_ref, *, chunk=4096):
    s = ft_ref.shape[1]
    row = lax.broadcasted_iota(jnp.int32, (LANES, chunk), 0)

    def body(c, carry):
        c0 = pl.multiple_of(c * chunk, chunk)
        rem = ft_ref[:, pl.ds(c0, chunk)] * LOG2E
        qf = jnp.where((row >= N_SPLIT) & (row < 2 * N_SPLIT), 1.0, 0.0)
        kf = jnp.where(row < N_SPLIT, 1.0, 0.0)
        for p in range(N_SPLIT):
            piece = rem.astype(BF16).astype(F32)
            rem = rem - piece
            qf = jnp.where(row == p, piece, qf)
            kf = jnp.where(row == N_SPLIT + p, -piece, kf)
        qf_ref[pl.ds(c0, chunk), :] = qf.astype(BF16).T
        kf_ref[pl.ds(c0, chunk), :] = kf.astype(BF16).T
        return carry

    lax.fori_loop(0, s // chunk, body, 0)


VA_ROWS = HEAD_DIM + 16
QUERY_CHUNK = 256


def _attn_kernel(q_ref, k_ref, vt_ref, ft_ref, z_ref, o_ref, qf_ref, kf_ref, vat_ref, qa_ref, sa_ref,
                 sb_ref, m_ref, acc_ref, *, tq, tk):
    i = pl.program_id(1)
    nt_dims = (((1,), (1,)), ((), ()))

    @pl.when(i == 0)
    def _():
        _decay_features(ft_ref, qf_ref, kf_ref)
        vat_ref[:HEAD_DIM, :] = vt_ref[...]
        row = lax.broadcasted_iota(jnp.int32, (VA_ROWS - HEAD_DIM, vat_ref.shape[1]), 0)
        vat_ref[HEAD_DIM:, :] = jnp.where(row == 0, 1.0, 0.0).astype(BF16)

    q0 = pl.multiple_of(i * tq, tq)
    qa_ref[:, :HEAD_DIM] = q_ref[...]
    qa_ref[:, HEAD_DIM:] = qf_ref[pl.ds(q0, tq), :]

    m_ref[...] = jnp.full(m_ref.shape, MASK_VALUE, F32)
    acc_ref[...] = jnp.zeros(acc_ref.shape, F32)

    def scores(j, dst_ref, c0=0):
        k0 = pl.multiple_of(j * tk, tk)
        ka = jnp.concatenate([k_ref[pl.ds(k0, tk), :], kf_ref[pl.ds(k0, tk), :]], axis=1)
        dst_ref[:, c0:] = lax.dot_general(ka, qa_ref[c0:, :], nt_dims,
                                          preferred_element_type=F32)

    def consume(src_ref, j, c0=0, causal=False):
        k0 = pl.multiple_of(j * tk, tk)
        row0 = pl.multiple_of(jnp.minimum(i, 0) * SUBLANES, SUBLANES)
        for q1 in range(c0, tq, QUERY_CHUNK):
            qs = slice(q1, q1 + QUERY_CHUNK)
            s2 = src_ref[pl.ds(row0, tk), qs]
            if causal and q1 - c0 < tk:
                keys = lax.broadcasted_iota(jnp.int32, (tk, QUERY_CHUNK), 0)
                qrys = lax.broadcasted_iota(jnp.int32, (tk, QUERY_CHUNK), 1) + (q1 - c0)
                s2 = jnp.where(qrys >= keys, s2, MASK_VALUE)
            m_prev = m_ref[:, qs]
            m_new = jnp.maximum(m_prev, jnp.max(s2, axis=0, keepdims=True))
            alpha = jnp.exp2(m_prev - m_new)
            p = jnp.exp2((s2 - m_new).astype(BF16))
            pv = jnp.dot(vat_ref[:, pl.ds(k0, tk)], p, preferred_element_type=F32)
            acc_ref[:, qs] = alpha * acc_ref[:, qs] + pv
            m_ref[:, qs] = m_new

    nk = tq // tk
    scores(0, sa_ref)

    def pair(jj):
        j = 2 * jj
        scores(j + 1, sb_ref)
        consume(sa_ref, j)
        scores(j + 2, sa_ref)
        consume(sb_ref, j + 1)

    def quad(qq, carry):
        pair(2 * qq)
        pair(2 * qq + 1)
        return carry

    n_pairs = i * (nk // 2)
    lax.fori_loop(0, n_pairs // 2, quad, 0)

    @pl.when(n_pairs % 2 == 1)
    def _():
        pair(n_pairs - 1)

    bufs = (sa_ref, sb_ref)
    for d in range(nk):
        if d + 1 < nk:
            scores(nk * i + d + 1, bufs[(d + 1) % 2], c0=(d + 1) * tk)
        consume(bufs[d % 2], nk * i + d, c0=d * tk, causal=True)

    acc = acc_ref[...]
    att = (acc[:HEAD_DIM, :] / acc[HEAD_DIM:HEAD_DIM + 1, :]).T
    o_ref[...] = (att * z_ref[...].astype(F32)).astype(o_ref.dtype)


def _attention(qk, vt, z, ft, *, heads, tq=1024, tk=512):
    s = qk.shape[0]
    assert tq % (2 * tk) == 0 and s % tq == 0
    return pl.pallas_call(
        functools.partial(_attn_kernel, tq=tq, tk=tk),
        grid=(heads, s // tq),
        in_specs=[pl.BlockSpec((tq, HEAD_DIM), lambda h, i: (i, h)),
                  pl.BlockSpec((s, HEAD_DIM), lambda h, i: (0, heads + h)),
                  pl.BlockSpec((HEAD_DIM, s), lambda h, i: (h, 0)),
                  pl.BlockSpec((None, 1, s), lambda h, i: (h, 0, 0)),
                  pl.BlockSpec((tq, HEAD_DIM), lambda h, i: (i, h))],
        out_specs=pl.BlockSpec((tq, HEAD_DIM), lambda h, i: (i, h)),
        out_shape=jax.ShapeDtypeStruct((s, heads * HEAD_DIM), BF16),
        scratch_shapes=[pltpu.VMEM((s, LANES), BF16),
                        pltpu.VMEM((s, LANES), BF16),
                        pltpu.VMEM((VA_ROWS, s), BF16),
                        pltpu.VMEM((tq, HEAD_DIM + LANES), BF16),
                        pltpu.VMEM((tk, tq), F32),
                        pltpu.VMEM((tk, tq), F32),
                        pltpu.VMEM((1, tq), F32),
                        pltpu.VMEM((VA_ROWS, tq), F32)],
        compiler_params=_params(("parallel", "arbitrary"), 48),
        name="attn",
    )(qk, qk, vt, ft, z)


def _rnn_kernel(x_ref, z_ref, cw_ref, cb_ref, wax_ref, ba_ref, bx_ref, lam_ref, o_ref,
                xpad_ref, hc_ref, a_ref, b_ref, *, tt, wt):
    t = pl.program_id(1)
    groups = tt // SUBLANES

    @pl.when(t == 0)
    def _():
        xpad_ref[:SUBLANES, :] = jnp.zeros((SUBLANES, wt), F32)
        hc_ref[...] = jnp.zeros(hc_ref.shape, F32)

    x = x_ref[...]
    xpad_ref[SUBLANES:, :] = x
    xc = cb_ref[...] + x * cw_ref[CONV_W - 1:CONV_W, :]
    for j in range(1, CONV_W):
        xc = xc + xpad_ref[pl.ds(SUBLANES - j, tt), :] * cw_ref[CONV_W - 1 - j:CONV_W - j, :]
    xpad_ref[:SUBLANES, :] = x[tt - SUBLANES:, :]

    xcb = xc.astype(BF16)
    ga, gx = [], []
    for n in range(wt // LANES):
        g = jnp.dot(xcb[:, n * LANES:(n + 1) * LANES], wax_ref[n], preferred_element_type=F32)
        ga.append(g[:, :LANES])
        gx.append(g[:, LANES:])
    r_gate = _sigmoid(jnp.concatenate(ga, axis=1) + ba_ref[...])
    i_gate = _sigmoid(jnp.concatenate(gx, axis=1) + bx_ref[...])
    log_a = (-RG_C) * r_gate * _softplus(-lam_ref[...])
    a = jnp.exp(log_a)
    mult = jnp.sqrt(-jnp.tanh(log_a) * (a * a + 1.0))
    b = mult * (i_gate * xc)

    a3 = a.reshape(groups, SUBLANES, wt)
    b3 = b.reshape(groups, SUBLANES, wt)
    sub = lax.broadcasted_iota(jnp.int32, (groups, SUBLANES, wt), 1)
    d = 1
    while d < SUBLANES:
        keep = sub >= d
        b3 = a3 * jnp.where(keep, pltpu.roll(b3, d, 1), 0.0) + b3
        a3 = a3 * jnp.where(keep, pltpu.roll(a3, d, 1), 1.0)
        d *= 2
    a_ref[...] = a3.reshape(tt, wt)
    b_ref[...] = b3.reshape(tt, wt)

    def body(g, hprev):
        g0 = pl.multiple_of(g * SUBLANES, SUBLANES)
        hg = b_ref[pl.ds(g0, SUBLANES), :] + a_ref[pl.ds(g0, SUBLANES), :] * hprev
        b_ref[pl.ds(g0, SUBLANES), :] = hg
        return jnp.broadcast_to(hg[SUBLANES - 1:SUBLANES, :], (SUBLANES, wt))

    hlast = lax.fori_loop(0, groups, body, hc_ref[...], unroll=8)
    hc_ref[...] = hlast
    o_ref[...] = (b_ref[...] * z_ref[...].astype(F32)).astype(o_ref.dtype)


def _rnn(x_r, z_r, conv_w, conv_b, w_ax, b_a, b_x, lam, *, tt=1024, wt=512):
    s, w = x_r.shape
    nb = wt // LANES
    tile = pl.BlockSpec((tt, wt), lambda n, t: (t, n))
    vec = pl.BlockSpec((1, wt), lambda n, t: (0, n))
    return pl.pallas_call(
        functools.partial(_rnn_kernel, tt=tt, wt=wt),
        grid=(w // wt, s // tt),
        in_specs=[tile, tile,
                  pl.BlockSpec((CONV_W, wt), lambda n, t: (0, n)), vec,
                  pl.BlockSpec((nb, LANES, 2 * LANES), lambda n, t: (n, 0, 0)),
                  vec, vec, vec],
        out_specs=tile,
        out_shape=jax.ShapeDtypeStruct((s, w), BF16),
        scratch_shapes=[pltpu.VMEM((SUBLANES + tt, wt), F32),
                        pltpu.VMEM((SUBLANES, wt), F32),
                        pltpu.VMEM((tt, wt), F32),
                        pltpu.VMEM((tt, wt), F32)],
        compiler_params=_params(("parallel", "arbitrary"), 40),
        name="rnn",
    )(x_r, z_r, conv_w, conv_b.reshape(1, w), w_ax, b_a.reshape(1, w), b_x.reshape(1, w),
      lam.reshape(1, w))


def _merge_kernel(ya_ref, yr_ref, ga_ref, gr_ref, wca_ref, wcr_ref, o_ref, wa_ref, wr_ref, *, ni, nc):
    s = pl.program_id(0)
    rows = wca_ref.shape[0]
    u = jnp.maximum(s - nc, 0)
    j = u // ni
    chunk = jnp.minimum(u % ni, nc - 1)

    def cast_chunk(slot, c):
        r0 = pl.multiple_of(c * rows, rows)
        wa_ref[slot, pl.ds(r0, rows), :] = wca_ref[...].astype(BF16)
        wr_ref[slot, pl.ds(r0, rows), :] = wcr_ref[...].astype(BF16)

    @pl.when(s < nc)
    def _():
        cast_chunk(0, s)

    @pl.when(s >= nc)
    def _():
        y_a = jnp.dot(ya_ref[...], wa_ref[j % 2], preferred_element_type=F32)
        acc = ga_ref[...].astype(F32) * y_a
        y_r = jnp.dot(yr_ref[...], wr_ref[j % 2], preferred_element_type=F32)
        o_ref[...] = (acc + gr_ref[...].astype(F32) * y_r).astype(o_ref.dtype)
        cast_chunk((j + 1) % 2, chunk)


def _merge(ya, yr, g, w_ba, w_br, *, tm=1024, tn=1024, nc=8):
    s, wd = ya.shape
    d = w_ba.shape[1]
    nj, ni = d // tn, s // tm
    assert d % tn == 0 and s % tm == 0 and wd % nc == 0 and ni >= nc
    step = functools.partial(_stream_step, ni=ni, nj=nj, nc=nc)
    act = pl.BlockSpec((tm, wd), lambda t: (step(t)[3], 0))
    wchunk = pl.BlockSpec((wd // nc, tn), lambda t: (step(t)[1], step(t)[0]))
    return pl.pallas_call(
        functools.partial(_merge_kernel, ni=ni, nc=nc),
        grid=(nc + nj * ni,),
        in_specs=[act, act,
                  pl.BlockSpec((tm, tn), lambda t: (step(t)[3], step(t)[2])),
                  pl.BlockSpec((tm, tn), lambda t: (step(t)[3], nj + step(t)[2])),
                  wchunk, wchunk],
        out_specs=pl.BlockSpec((tm, tn), lambda t: (step(t)[3], step(t)[2])),
        out_shape=jax.ShapeDtypeStruct((s, d), BF16),
        scratch_shapes=[pltpu.VMEM((2, wd, tn), BF16), pltpu.VMEM((2, wd, tn), BF16)],
        compiler_params=_params(("arbitrary",), 60),
        name="merge",
    )(ya, yr, g, g, w_ba, w_br)


def kernel(x, c, w_ada, b_ada, norm_w, w_in, b_f, q_norm_w, k_norm_w, conv_w, conv_b, w_rg_a,
           b_rg_a, w_rg_x, b_rg_x, lru_lambda, w_br_attn, w_br_rnn, w_gate, b_gate, w_out):
    bsz, s, d = x.shape
    assert bsz == 1
    aw = w_br_attn.shape[0]
    rw = w_br_rnn.shape[0]
    heads = aw // HEAD_DIM
    x2 = x.reshape(s, d)

    mod = _ada(c, w_ada, b_ada)
    shift, scale, gate_res = mod[:, :d], mod[:, d:2 * d], mod[:, 2 * d:]
    h = _prenorm(x2, norm_w, scale, shift)

    o_f = 4 * aw
    o_xr = o_f + heads
    qk_gain = jnp.concatenate([jnp.tile(q_norm_w * (LOG2E / math.sqrt(HEAD_DIM)), heads),
                               jnp.tile(k_norm_w, heads)])
    proj = functools.partial(_proj_stream, h, w_in.T, w_transposed=True, tm=1024, tn=1024)
    qk = proj(n=2 * aw, col0=0, out_dtype=BF16, epilogue="headnorm", row_vecs=(qk_gain,),
              name="proj_qk")
    vt = proj(n=aw, col0=2 * aw, out_dtype=BF16, epilogue="plain", out_transposed=True,
              name="proj_v")
    z_a = proj(n=aw, col0=3 * aw, out_dtype=BF16, epilogue="silu", name="proj_za")
    logf = _proj(h, w_in.T, w_transposed=True, n=LANES, col0=o_f, out_dtype=F32,
                 epilogue="logsigmoid", tm=1024, tn=LANES,
                 row_vecs=(jnp.pad(b_f, (0, LANES - heads)),), name="proj_f")
    x_r = proj(n=rw, col0=o_xr, out_dtype=F32, epilogue="plain", name="proj_xr")
    z_r = proj(n=rw, col0=o_xr + rw, out_dtype=BF16, epilogue="silu", name="proj_zr")

    ft = _cumsum_t(logf)[:heads].reshape(heads, 1, s)
    ya = _attention(qk, vt, z_a, ft, heads=heads)

    w_ax = jnp.concatenate([w_rg_a, w_rg_x], axis=2).astype(BF16)
    yr = _rnn(x_r, z_r, conv_w, conv_b, w_ax, b_rg_a, b_rg_x, lru_lambda)

    g = _proj_stream(h, w_gate, n=2 * d, col0=0, out_dtype=BF16, epilogue="sigmoid_bias", tm=1024,
                     tn=1024, row_vecs=(b_gate,), name="gate")
    merged = _merge(ya, yr, g, w_br_attn, w_br_rnn)
    y = _proj_stream(merged, w_out, n=d, col0=0, out_dtype=x.dtype, epilogue="residual", tm=1024,
                     tn=1024, tiles=(x2,), row_vecs=(gate_res.reshape(-1),), nc=8, vmem_mib=60,
                     name="out")
    return y.reshape(bsz, s, d)
```

```python
import functools
import math

import jax
import jax.numpy as jnp
from jax import lax
from jax.experimental import pallas as pl
from jax.experimental.pallas import tpu as pltpu

F32 = jnp.float32
BF16 = jnp.bfloat16

LANES = 128
SUBLANES = 8
HEAD_DIM = 128
CONV_W = 4
RG_C = 8.0
EPS = 1e-6
MASK_VALUE = -1e30
LOG2E = 1.4426950408889634
MIB = 1024 * 1024


def _params(semantics, vmem_mib, flags=None):
    return pltpu.CompilerParams(dimension_semantics=semantics,
                                vmem_limit_bytes=vmem_mib * MIB, flags=flags)


def _sigmoid(x):
    return 0.5 * jnp.tanh(0.5 * x) + 0.5


def _silu(x):
    return x * _sigmoid(x)


def _softplus(x):
    return jnp.maximum(x, 0.0) + jnp.log1p(jnp.exp(-jnp.abs(x)))


def _ada_kernel(c_ref, w_ref, b_ref, o_ref, *, rows):
    d = w_ref.shape[0]

    def body(r, acc):
        r0 = pl.multiple_of(r * rows, rows)
        c = c_ref[pl.ds(r0, rows), :]
        return acc + jnp.sum(w_ref[pl.ds(r0, rows), :] * _silu(c), axis=0, keepdims=True)

    acc = lax.fori_loop(0, d // rows, body, jnp.zeros(o_ref.shape, F32))
    o_ref[...] = acc + b_ref[...]


def _ada(c, w_ada, b_ada, *, tn=512, rows=512):
    d, n = w_ada.shape
    return pl.pallas_call(
        functools.partial(_ada_kernel, rows=rows),
        grid=(n // tn,),
        in_specs=[pl.BlockSpec((d, 1), lambda j: (0, 0)),
                  pl.BlockSpec((d, tn), lambda j: (0, j)),
                  pl.BlockSpec((1, tn), lambda j: (0, j))],
        out_specs=pl.BlockSpec((1, tn), lambda j: (0, j)),
        out_shape=jax.ShapeDtypeStruct((1, n), F32),
        compiler_params=_params(("parallel",), 40),
        name="ada",
    )(c.reshape(d, 1), w_ada, b_ada.reshape(1, n))


def _prenorm_kernel(x_ref, nw_ref, scale_ref, shift_ref, wf_ref, bf_ref, h_ref, lf_ref, wfb_ref):
    @pl.when(pl.program_id(0) == 0)
    def _():
        wfb_ref[...] = wf_ref[...].astype(BF16)

    x = x_ref[...]
    ms = jnp.mean(x * x, axis=-1, keepdims=True)
    h = x * lax.rsqrt(ms + EPS) * nw_ref[...]
    h = (h * (1.0 + scale_ref[...]) + shift_ref[...]).astype(h_ref.dtype)
    h_ref[...] = h
    z = lax.dot_general(h, wfb_ref[...], (((1,), (1,)), ((), ())),
                        preferred_element_type=F32) + bf_ref[...]
    lf_ref[...] = jnp.minimum(z, 0.0) - jnp.log1p(jnp.exp(-jnp.abs(z)))


def _prenorm(x, norm_w, scale, shift, w_t, col_f, b_f, *, tm=512):
    s, d = x.shape
    assert col_f % SUBLANES == 0
    vec = pl.BlockSpec((1, d), lambda i: (0, 0))
    b_pad = jnp.pad(b_f, (0, LANES - b_f.shape[0])).reshape(1, LANES)
    return pl.pallas_call(
        _prenorm_kernel,
        grid=(s // tm,),
        in_specs=[pl.BlockSpec((tm, d), lambda i: (i, 0)), vec, vec, vec,
                  pl.BlockSpec((pl.Element(LANES), pl.Element(d)), lambda i: (col_f, 0)),
                  pl.BlockSpec((1, LANES), lambda i: (0, 0))],
        out_specs=(pl.BlockSpec((tm, d), lambda i: (i, 0)),
                   pl.BlockSpec((tm, LANES), lambda i: (i, 0))),
        out_shape=(jax.ShapeDtypeStruct((s, d), BF16), jax.ShapeDtypeStruct((s, LANES), F32)),
        scratch_shapes=[pltpu.VMEM((LANES, d), BF16)],
        compiler_params=_params(("arbitrary",), 40),
        name="prenorm",
    )(x, norm_w.reshape(1, d), scale, shift, w_t, b_pad)


def _epilogue(y, rest, o_ref, epilogue):
    if epilogue == "plain":
        o_ref[...] = y.astype(o_ref.dtype)
    elif epilogue == "silu":
        o_ref[...] = _silu(y).astype(o_ref.dtype)
    elif epilogue == "sigmoid_bias":
        o_ref[...] = _sigmoid(y + rest[0][...]).astype(o_ref.dtype)
    elif epilogue == "headnorm":
        gain_ref = rest[0]
        for hh in range(y.shape[1] // HEAD_DIM):
            sl = slice(hh * HEAD_DIM, (hh + 1) * HEAD_DIM)
            yh = y[:, sl]
            ms = jnp.mean(yh * yh, axis=-1, keepdims=True)
            o_ref[:, sl] = (yh * lax.rsqrt(ms + EPS) * gain_ref[:, sl]).astype(o_ref.dtype)
    elif epilogue == "residual":
        x_ref, g_ref = rest[0], rest[1]
        o_ref[...] = (x_ref[...] + g_ref[...] * y).astype(o_ref.dtype)
    else:
        raise ValueError(epilogue)


W_CHUNKS = 4


def _stream_step(s, *, ni, nj, nc):
    u = jnp.maximum(s - nc, 0)
    j, i = u // ni, u % ni
    pro = s < nc
    last = j == nj - 1
    tile = jnp.where(pro, 0, jnp.minimum(j + 1, nj - 1))
    chunk = jnp.where(pro, s, jnp.where(last, nc - 1, jnp.minimum(i, nc - 1)))
    return tile, chunk, j, i


def _proj_stream_kernel(a_ref, wc_ref, *rest, epilogue, w_transposed, out_transposed, ni, nc):
    wb_ref = rest[-1]
    o_ref = rest[-2]
    s = pl.program_id(0)
    rows = wc_ref.shape[0]
    u = jnp.maximum(s - nc, 0)
    j = u // ni
    chunk = jnp.minimum(u % ni, nc - 1)

    def cast_chunk(slot, c):
        r0 = pl.multiple_of(c * rows, rows)
        wb_ref[slot, pl.ds(r0, rows), :] = wc_ref[...].astype(BF16)

    @pl.when(s < nc)
    def _():
        cast_chunk(0, s)

    @pl.when(s >= nc)
    def _():
        wb = wb_ref[j % 2]
        nt_dims = (((1,), (1,)), ((), ()))
        if out_transposed:
            y = lax.dot_general(wb, a_ref[...], nt_dims, preferred_element_type=F32)
        elif w_transposed:
            y = lax.dot_general(a_ref[...], wb, nt_dims, preferred_element_type=F32)
        else:
            y = jnp.dot(a_ref[...], wb, preferred_element_type=F32)
        _epilogue(y, rest[:-2], o_ref, epilogue)
        cast_chunk((j + 1) % 2, chunk)


def _proj_stream(a, w, *, n, col0, out_dtype, epilogue, tm, tn, w_transposed=False,
                 out_transposed=False, row_vecs=(), tiles=(), nc=W_CHUNKS, vmem_mib=56, name):
    m, k = a.shape
    nj, ni = n // tn, m // tm
    assert n % tn == 0 and m % tm == 0 and ni >= nc
    assert not out_transposed or (w_transposed and not row_vecs and not tiles)

    where_to = functools.partial(_stream_step, ni=ni, nj=nj, nc=nc)

    if w_transposed:
        assert col0 % SUBLANES == 0 and tn % nc == 0
        cr = tn // nc

        def w_index(s):
            tile, chunk, _, _ = where_to(s)
            return pl.multiple_of(col0 + tile * tn + chunk * cr, SUBLANES), 0

        w_spec = pl.BlockSpec((pl.Element(cr), pl.Element(k)), w_index)
        wb_shape = (2, tn, k)
    else:
        assert col0 % tn == 0 and k % nc == 0
        ck = k // nc

        def w_index(s):
            tile, chunk, _, _ = where_to(s)
            return chunk, col0 // tn + tile

        w_spec = pl.BlockSpec((ck, tn), w_index)
        wb_shape = (2, k, tn)

    def a_index(s):
        _, _, _, i = where_to(s)
        return i, 0

    def vec_index(s):
        _, _, j, _ = where_to(s)
        return 0, j

    def out_index(s):
        _, _, j, i = where_to(s)
        return (j, i) if out_transposed else (i, j)

    in_specs = [pl.BlockSpec((tm, k), a_index), w_spec]
    args = [a, w]
    for t in tiles:
        in_specs.append(pl.BlockSpec((tm, tn), out_index))
        args.append(t)
    for v in row_vecs:
        in_specs.append(pl.BlockSpec((1, tn), vec_index))
        args.append(v.reshape(1, n))
    if out_transposed:
        out_spec = pl.BlockSpec((tn, tm), out_index)
        out_shape = jax.ShapeDtypeStruct((n, m), out_dtype)
    else:
        out_spec = pl.BlockSpec((tm, tn), out_index)
        out_shape = jax.ShapeDtypeStruct((m, n), out_dtype)
    return pl.pallas_call(
        functools.partial(_proj_stream_kernel, epilogue=epilogue, w_transposed=w_transposed,
                          out_transposed=out_transposed, ni=ni, nc=nc),
        grid=(nc + nj * ni,),
        in_specs=in_specs,
        out_specs=out_spec,
        out_shape=out_shape,
        scratch_shapes=[pltpu.VMEM(wb_shape, BF16)],
        compiler_params=_params(("arbitrary",), vmem_mib),
        name=name,
    )(*args)


def _cumsum_kernel(x_ref, ft_ref, *, chunk):
    s = x_ref.shape[0]
    row = lax.broadcasted_iota(jnp.int32, (chunk, LANES), 0)

    def body(c, carry):
        s0 = pl.multiple_of(c * chunk, chunk)
        x = x_ref[pl.ds(s0, chunk), :]
        d = 1
        while d < chunk:
            x = x + jnp.where(row >= d, pltpu.roll(x, d, 0), 0.0)
            d *= 2
        x = x + carry
        ft_ref[:, pl.ds(s0, chunk)] = x.T
        return x[chunk - 1:chunk, :]

    lax.fori_loop(0, s // chunk, body, jnp.zeros((1, LANES), F32))


def _cumsum_t(logf, *, chunk=512):
    s = logf.shape[0]
    return pl.pallas_call(
        functools.partial(_cumsum_kernel, chunk=chunk),
        out_shape=jax.ShapeDtypeStruct((LANES, s), F32),
        compiler_params=_params(None, 40),
        name="cumsum",
    )(logf)


N_SPLIT = 3


def _decay_features(ft_ref, qf_ref, kf_ref, *, chunk=4096):
    s = ft_ref.shape[1]
    row = lax.broadcasted_iota(jnp.int32, (LANES, chunk), 0)

    def body(c, carry):
        c0 = pl.multiple_of(c * chunk, chunk)
        rem = ft_ref[:, pl.ds(c0, chunk)] * LOG2E
        qf = jnp.where((row >= N_SPLIT) & (row < 2 * N_SPLIT), 1.0, 0.0)
        kf = jnp.where(row < N_SPLIT, 1.0, 0.0)
        for p in range(N_SPLIT):
            piece = rem.astype(BF16).astype(F32)
            rem = rem - piece
            qf = jnp.where(row == p, piece, qf)
            kf = jnp.where(row == N_SPLIT + p, -piece, kf)
        qf_ref[pl.ds(c0, chunk), :] = qf.astype(BF16).T
        kf_ref[pl.ds(c0, chunk), :] = kf.astype(BF16).T
        return carry

    lax.fori_loop(0, s // chunk, body, 0)


VA_ROWS = HEAD_DIM + 16
QUERY_CHUNK = 256


def _attn_kernel(q_ref, k_ref, vt_ref, ft_ref, z_ref, o_ref, qf_ref, kf_ref, vat_ref, qa_ref, sa_ref,
                 sb_ref, m_ref, acc_ref, *, tq, tk):
    i = pl.program_id(1)
    nt_dims = (((1,), (1,)), ((), ()))

    @pl.when(i == 0)
    def _():
        _decay_features(ft_ref, qf_ref, kf_ref)
        vat_ref[:HEAD_DIM, :] = vt_ref[...]
        row = lax.broadcasted_iota(jnp.int32, (VA_ROWS - HEAD_DIM, vat_ref.shape[1]), 0)
        vat_ref[HEAD_DIM:, :] = jnp.where(row == 0, 1.0, 0.0).astype(BF16)

    q0 = pl.multiple_of(i * tq, tq)
    qa_ref[:, :HEAD_DIM] = q_ref[...]
    qa_ref[:, HEAD_DIM:] = qf_ref[pl.ds(q0, tq), :]

    m_ref[...] = jnp.full(m_ref.shape, MASK_VALUE, F32)
    acc_ref[...] = jnp.zeros(acc_ref.shape, F32)

    def scores(j, dst_ref, c0=0):
        k0 = pl.multiple_of(j * tk, tk)
        ka = jnp.concatenate([k_ref[pl.ds(k0, tk), :], kf_ref[pl.ds(k0, tk), :]], axis=1)
        dst_ref[:, c0:] = lax.dot_general(ka, qa_ref[c0:, :], nt_dims,
                                          preferred_element_type=F32)

    def consume(src_ref, j, c0=0, causal=False):
        k0 = pl.multiple_of(j * tk, tk)
        row0 = pl.multiple_of(jnp.minimum(i, 0) * SUBLANES, SUBLANES)
        for q1 in range(c0, tq, QUERY_CHUNK):
            qs = slice(q1, q1 + QUERY_CHUNK)
            s2 = src_ref[pl.ds(row0, tk), qs]
            if causal and q1 - c0 < tk:
                keys = lax.broadcasted_iota(jnp.int32, (tk, QUERY_CHUNK), 0)
                qrys = lax.broadcasted_iota(jnp.int32, (tk, QUERY_CHUNK), 1) + (q1 - c0)
                s2 = jnp.where(qrys >= keys, s2, MASK_VALUE)
            m_prev = m_ref[:, qs]
            m_new = jnp.maximum(m_prev, jnp.max(s2, axis=0, keepdims=True))
            alpha = jnp.exp2(m_prev - m_new)
            p = jnp.exp2((s2 - m_new).astype(BF16))
            pv = jnp.dot(vat_ref[:, pl.ds(k0, tk)], p, preferred_element_type=F32)
            acc_ref[:, qs] = alpha * acc_ref[:, qs] + pv
            m_ref[:, qs] = m_new

    nk = tq // tk
    scores(0, sa_ref)

    def pair(jj):
        j = 2 * jj
        scores(j + 1, sb_ref)
        consume(sa_ref, j)
        scores(j + 2, sa_ref)
        consume(sb_ref, j + 1)

    def quad(qq, carry):
        pair(2 * qq)
        pair(2 * qq + 1)
        return carry

    n_pairs = i * (nk // 2)
    lax.fori_loop(0, n_pairs // 2, quad, 0)

    @pl.when(n_pairs % 2 == 1)
    def _():
        pair(n_pairs - 1)

    bufs = (sa_ref, sb_ref)
    for d in range(nk):
        if d + 1 < nk:
            scores(nk * i + d + 1, bufs[(d + 1) % 2], c0=(d + 1) * tk)
        consume(bufs[d % 2], nk * i + d, c0=d * tk, causal=True)

    acc = acc_ref[...]
    att = (acc[:HEAD_DIM, :] / acc[HEAD_DIM:HEAD_DIM + 1, :]).T
    o_ref[...] = (att * z_ref[...].astype(F32)).astype(o_ref.dtype)


def _attention(qk, vt, z, ft, *, heads, tq=1024, tk=512):
    s = qk.shape[0]
    assert tq % (2 * tk) == 0 and s % tq == 0
    return pl.pallas_call(
        functools.partial(_attn_kernel, tq=tq, tk=tk),
        grid=(heads, s // tq),
        in_specs=[pl.BlockSpec((tq, HEAD_DIM), lambda h, i: (i, h)),
                  pl.BlockSpec((s, HEAD_DIM), lambda h, i: (0, heads + h)),
                  pl.BlockSpec((HEAD_DIM, s), lambda h, i: (h, 0)),
                  pl.BlockSpec((None, 1, s), lambda h, i: (h, 0, 0)),
                  pl.BlockSpec((tq, HEAD_DIM), lambda h, i: (i, h))],
        out_specs=pl.BlockSpec((tq, HEAD_DIM), lambda h, i: (i, h)),
        out_shape=jax.ShapeDtypeStruct((s, heads * HEAD_DIM), BF16),
        scratch_shapes=[pltpu.VMEM((s, LANES), BF16),
                        pltpu.VMEM((s, LANES), BF16),
                        pltpu.VMEM((VA_ROWS, s), BF16),
                        pltpu.VMEM((tq, HEAD_DIM + LANES), BF16),
                        pltpu.VMEM((tk, tq), F32),
                        pltpu.VMEM((tk, tq), F32),
                        pltpu.VMEM((1, tq), F32),
                        pltpu.VMEM((VA_ROWS, tq), F32)],
        compiler_params=_params(("parallel", "arbitrary"), 48),
        name="attn",
    )(qk, qk, vt, ft, z)


def _rnn_kernel(x_ref, z_ref, cw_ref, cb_ref, wax_ref, ba_ref, bx_ref, lam_ref, o_ref,
                xpad_ref, hc_ref, a_ref, b_ref, *, tt, wt):
    t = pl.program_id(1)
    groups = tt // SUBLANES

    @pl.when(t == 0)
    def _():
        xpad_ref[:SUBLANES, :] = jnp.zeros((SUBLANES, wt), F32)
        hc_ref[...] = jnp.zeros(hc_ref.shape, F32)

    x = x_ref[...]
    xpad_ref[SUBLANES:, :] = x
    xc = cb_ref[...] + x * cw_ref[CONV_W - 1:CONV_W, :]
    for j in range(1, CONV_W):
        xc = xc + xpad_ref[pl.ds(SUBLANES - j, tt), :] * cw_ref[CONV_W - 1 - j:CONV_W - j, :]
    xpad_ref[:SUBLANES, :] = x[tt - SUBLANES:, :]

    xcb = xc.astype(BF16)
    ga, gx = [], []
    for n in range(wt // LANES):
        g = jnp.dot(xcb[:, n * LANES:(n + 1) * LANES], wax_ref[n], preferred_element_type=F32)
        ga.append(g[:, :LANES])
        gx.append(g[:, LANES:])
    r_gate = _sigmoid(jnp.concatenate(ga, axis=1) + ba_ref[...])
    i_gate = _sigmoid(jnp.concatenate(gx, axis=1) + bx_ref[...])
    log_a = (-RG_C) * r_gate * _softplus(-lam_ref[...])
    a = jnp.exp(log_a)
    mult = jnp.sqrt(-jnp.tanh(log_a) * (a * a + 1.0))
    b = mult * (i_gate * xc)

    a3 = a.reshape(groups, SUBLANES, wt)
    b3 = b.reshape(groups, SUBLANES, wt)
    sub = lax.broadcasted_iota(jnp.int32, (groups, SUBLANES, wt), 1)
    d = 1
    while d < SUBLANES:
        keep = sub >= d
        b3 = a3 * jnp.where(keep, pltpu.roll(b3, d, 1), 0.0) + b3
        a3 = a3 * jnp.where(keep, pltpu.roll(a3, d, 1), 1.0)
        d *= 2
    a_ref[...] = a3.reshape(tt, wt)
    b_ref[...] = b3.reshape(tt, wt)

    def body(g, hprev):
        g0 = pl.multiple_of(g * SUBLANES, SUBLANES)
        hg = b_ref[pl.ds(g0, SUBLANES), :] + a_ref[pl.ds(g0, SUBLANES), :] * hprev
        b_ref[pl.ds(g0, SUBLANES), :] = hg
        return jnp.broadcast_to(hg[SUBLANES - 1:SUBLANES, :], (SUBLANES, wt))

    hlast = lax.fori_loop(0, groups, body, hc_ref[...], unroll=8)
    hc_ref[...] = hlast
    o_ref[...] = (b_ref[...] * z_ref[...].astype(F32)).astype(o_ref.dtype)


def _rnn(x_r, z_r, conv_w, conv_b, w_ax, b_a, b_x, lam, *, tt=1024, wt=512):
    s, w = x_r.shape
    nb = wt // LANES
    tile = pl.BlockSpec((tt, wt), lambda n, t: (t, n))
    vec = pl.BlockSpec((1, wt), lambda n, t: (0, n))
    return pl.pallas_call(
        functools.partial(_rnn_kernel, tt=tt, wt=wt),
        grid=(w // wt, s // tt),
        in_specs=[tile, tile,
                  pl.BlockSpec((CONV_W, wt), lambda n, t: (0, n)), vec,
                  pl.BlockSpec((nb, LANES, 2 * LANES), lambda n, t: (n, 0, 0)),
                  vec, vec, vec],
        out_specs=tile,
        out_shape=jax.ShapeDtypeStruct((s, w), BF16),
        scratch_shapes=[pltpu.VMEM((SUBLANES + tt, wt), F32),
                        pltpu.VMEM((SUBLANES, wt), F32),
                        pltpu.VMEM((tt, wt), F32),
                        pltpu.VMEM((tt, wt), F32)],
        compiler_params=_params(("parallel", "arbitrary"), 40),
        name="rnn",
    )(x_r, z_r, conv_w, conv_b.reshape(1, w), w_ax, b_a.reshape(1, w), b_x.reshape(1, w),
      lam.reshape(1, w))


def _merge_kernel(ya_ref, yr_ref, ga_ref, gr_ref, wca_ref, wcr_ref, o_ref, wa_ref, wr_ref, *, ni, nc):
    s = pl.program_id(0)
    rows = wca_ref.shape[0]
    u = jnp.maximum(s - nc, 0)
    j = u // ni
    chunk = jnp.minimum(u % ni, nc - 1)

    def cast_chunk(slot, c):
        r0 = pl.multiple_of(c * rows, rows)
        wa_ref[slot, pl.ds(r0, rows), :] = wca_ref[...].astype(BF16)
        wr_ref[slot, pl.ds(r0, rows), :] = wcr_ref[...].astype(BF16)

    @pl.when(s < nc)
    def _():
        cast_chunk(0, s)

    @pl.when(s >= nc)
    def _():
        y_a = jnp.dot(ya_ref[...], wa_ref[j % 2], preferred_element_type=F32)
        acc = ga_ref[...].astype(F32) * y_a
        y_r = jnp.dot(yr_ref[...], wr_ref[j % 2], preferred_element_type=F32)
        o_ref[...] = (acc + gr_ref[...].astype(F32) * y_r).astype(o_ref.dtype)
        cast_chunk((j + 1) % 2, chunk)


def _merge(ya, yr, g, w_ba, w_br, *, tm=1024, tn=1024, nc=8):
    s, wd = ya.shape
    d = w_ba.shape[1]
    nj, ni = d // tn, s // tm
    assert d % tn == 0 and s % tm == 0 and wd % nc == 0 and ni >= nc
    step = functools.partial(_stream_step, ni=ni, nj=nj, nc=nc)
    act = pl.BlockSpec((tm, wd), lambda t: (step(t)[3], 0))
    wchunk = pl.BlockSpec((wd // nc, tn), lambda t: (step(t)[1], step(t)[0]))
    return pl.pallas_call(
        functools.partial(_merge_kernel, ni=ni, nc=nc),
        grid=(nc + nj * ni,),
        in_specs=[act, act,
                  pl.BlockSpec((tm, tn), lambda t: (step(t)[3], step(t)[2])),
                  pl.BlockSpec((tm, tn), lambda t: (step(t)[3], nj + step(t)[2])),
                  wchunk, wchunk],
        out_specs=pl.BlockSpec((tm, tn), lambda t: (step(t)[3], step(t)[2])),
        out_shape=jax.ShapeDtypeStruct((s, d), BF16),
        scratch_shapes=[pltpu.VMEM((2, wd, tn), BF16), pltpu.VMEM((2, wd, tn), BF16)],
        compiler_params=_params(("arbitrary",), 60),
        name="merge",
    )(ya, yr, g, g, w_ba, w_br)


def kernel(x, c, w_ada, b_ada, norm_w, w_in, b_f, q_norm_w, k_norm_w, conv_w, conv_b, w_rg_a,
           b_rg_a, w_rg_x, b_rg_x, lru_lambda, w_br_attn, w_br_rnn, w_gate, b_gate, w_out):
    bsz, s, d = x.shape
    assert bsz == 1
    aw = w_br_attn.shape[0]
    rw = w_br_rnn.shape[0]
    heads = aw // HEAD_DIM
    x2 = x.reshape(s, d)

    mod = _ada(c, w_ada, b_ada)
    shift, scale, gate_res = mod[:, :d], mod[:, d:2 * d], mod[:, 2 * d:]
    o_f = 4 * aw
    o_xr = o_f + heads
    h, logf = _prenorm(x2, norm_w, scale, shift, w_in.T, o_f, b_f)
    qk_gain = jnp.concatenate([jnp.tile(q_norm_w * (LOG2E / math.sqrt(HEAD_DIM)), heads),
                               jnp.tile(k_norm_w, heads)])
    proj = functools.partial(_proj_stream, h, w_in.T, w_transposed=True, tm=1024, tn=1024)
    qk = proj(n=2 * aw, col0=0, out_dtype=BF16, epilogue="headnorm", row_vecs=(qk_gain,),
              name="proj_qk")
    vt = proj(n=aw, col0=2 * aw, out_dtype=BF16, epilogue="plain", out_transposed=True,
              name="proj_v")
    z_a = proj(n=aw, col0=3 * aw, out_dtype=BF16, epilogue="silu", name="proj_za")
    x_r = proj(n=rw, col0=o_xr, out_dtype=F32, epilogue="plain", name="proj_xr")
    z_r = proj(n=rw, col0=o_xr + rw, out_dtype=BF16, epilogue="silu", name="proj_zr")

    ft = _cumsum_t(logf)[:heads].reshape(heads, 1, s)
    ya = _attention(qk, vt, z_a, ft, heads=heads)

    w_ax = jnp.concatenate([w_rg_a, w_rg_x], axis=2).astype(BF16)
    yr = _rnn(x_r, z_r, conv_w, conv_b, w_ax, b_rg_a, b_rg_x, lru_lambda)

    g = _proj_stream(h, w_gate, n=2 * d, col0=0, out_dtype=BF16, epilogue="sigmoid_bias", tm=1024,
                     tn=1024, row_vecs=(b_gate,), name="gate")
    merged = _merge(ya, yr, g, w_br_attn, w_br_rnn)
    y = _proj_stream(merged, w_out, n=d, col0=0, out_dtype=x.dtype, epilogue="residual", tm=1024,
                     tn=1024, tiles=(x2,), row_vecs=(gate_res.reshape(-1),), nc=8, vmem_mib=60,
                     name="out")
    return y.reshape(bsz, s, d)
```

```python
import functools
import math

import jax
import jax.numpy as jnp
from jax import lax
from jax.experimental import pallas as pl
from jax.experimental.pallas import tpu as pltpu

F32 = jnp.float32
BF16 = jnp.bfloat16

LANES = 128
SUBLANES = 8
HEAD_DIM = 128
CONV_W = 4
RG_C = 8.0
EPS = 1e-6
MASK_VALUE = -1e30
LOG2E = 1.4426950408889634
MIB = 1024 * 1024


def _params(semantics, vmem_mib, flags=None):
    return pltpu.CompilerParams(dimension_semantics=semantics,
                                vmem_limit_bytes=vmem_mib * MIB, flags=flags)


def _sigmoid(x):
    return 0.5 * jnp.tanh(0.5 * x) + 0.5


def _silu(x):
    return x * _sigmoid(x)


def _softplus(x):
    return jnp.maximum(x, 0.0) + jnp.log1p(jnp.exp(-jnp.abs(x)))


def _ada_kernel(c_ref, w_ref, b_ref, o_ref, *, rows):
    d = w_ref.shape[0]

    def body(r, acc):
        r0 = pl.multiple_of(r * rows, rows)
        c = c_ref[pl.ds(r0, rows), :]
        return acc + jnp.sum(w_ref[pl.ds(r0, rows), :] * _silu(c), axis=0, keepdims=True)

    acc = lax.fori_loop(0, d // rows, body, jnp.zeros(o_ref.shape, F32))
    o_ref[...] = acc + b_ref[...]


def _ada(c, w_ada, b_ada, *, tn=512, rows=512):
    d, n = w_ada.shape
    return pl.pallas_call(
        functools.partial(_ada_kernel, rows=rows),
        grid=(n // tn,),
        in_specs=[pl.BlockSpec((d, 1), lambda j: (0, 0)),
                  pl.BlockSpec((d, tn), lambda j: (0, j)),
                  pl.BlockSpec((1, tn), lambda j: (0, j))],
        out_specs=pl.BlockSpec((1, tn), lambda j: (0, j)),
        out_shape=jax.ShapeDtypeStruct((1, n), F32),
        compiler_params=_params(("parallel",), 40),
        name="ada",
    )(c.reshape(d, 1), w_ada, b_ada.reshape(1, n))


def _prenorm_kernel(x_ref, nw_ref, scale_ref, shift_ref, wf_ref, bf_ref, h_ref, lf_ref, wfb_ref):
    @pl.when(pl.program_id(0) == 0)
    def _():
        wfb_ref[...] = wf_ref[...].astype(BF16)

    x = x_ref[...]
    ms = jnp.mean(x * x, axis=-1, keepdims=True)
    h = x * lax.rsqrt(ms + EPS) * nw_ref[...]
    h = (h * (1.0 + scale_ref[...]) + shift_ref[...]).astype(h_ref.dtype)
    h_ref[...] = h
    z = lax.dot_general(h, wfb_ref[...], (((1,), (1,)), ((), ())),
                        preferred_element_type=F32) + bf_ref[...]
    lf_ref[...] = jnp.minimum(z, 0.0) - jnp.log1p(jnp.exp(-jnp.abs(z)))


def _prenorm(x, norm_w, scale, shift, w_t, col_f, b_f, *, tm=512):
    s, d = x.shape
    assert col_f % SUBLANES == 0
    vec = pl.BlockSpec((1, d), lambda i: (0, 0))
    b_pad = jnp.pad(b_f, (0, LANES - b_f.shape[0])).reshape(1, LANES)
    return pl.pallas_call(
        _prenorm_kernel,
        grid=(s // tm,),
        in_specs=[pl.BlockSpec((tm, d), lambda i: (i, 0)), vec, vec, vec,
                  pl.BlockSpec((pl.Element(LANES), pl.Element(d)), lambda i: (col_f, 0)),
                  pl.BlockSpec((1, LANES), lambda i: (0, 0))],
        out_specs=(pl.BlockSpec((tm, d), lambda i: (i, 0)),
                   pl.BlockSpec((tm, LANES), lambda i: (i, 0))),
        out_shape=(jax.ShapeDtypeStruct((s, d), BF16), jax.ShapeDtypeStruct((s, LANES), F32)),
        scratch_shapes=[pltpu.VMEM((LANES, d), BF16)],
        compiler_params=_params(("arbitrary",), 40),
        name="prenorm",
    )(x, norm_w.reshape(1, d), scale, shift, w_t, b_pad)


def _epilogue(y, rest, o_ref, epilogue):
    if epilogue == "plain":
        o_ref[...] = y.astype(o_ref.dtype)
    elif epilogue == "silu":
        o_ref[...] = _silu(y).astype(o_ref.dtype)
    elif epilogue == "sigmoid_bias":
        o_ref[...] = _sigmoid(y + rest[0][...]).astype(o_ref.dtype)
    elif epilogue == "headnorm":
        gain_ref = rest[0]
        for hh in range(y.shape[1] // HEAD_DIM):
            sl = slice(hh * HEAD_DIM, (hh + 1) * HEAD_DIM)
            yh = y[:, sl]
            ms = jnp.mean(yh * yh, axis=-1, keepdims=True)
            o_ref[:, sl] = (yh * lax.rsqrt(ms + EPS) * gain_ref[:, sl]).astype(o_ref.dtype)
    elif epilogue == "residual":
        x_ref, g_ref = rest[0], rest[1]
        o_ref[...] = (x_ref[...] + g_ref[...] * y).astype(o_ref.dtype)
    else:
        raise ValueError(epilogue)


W_CHUNKS = 4


def _stream_step(s, *, ni, nj, nc):
    u = jnp.maximum(s - nc, 0)
    j, i = u // ni, u % ni
    pro = s < nc
    last = j == nj - 1
    tile = jnp.where(pro, 0, jnp.minimum(j + 1, nj - 1))
    chunk = jnp.where(pro, s, jnp.where(last, nc - 1, jnp.minimum(i, nc - 1)))
    return tile, chunk, j, i


def _proj_stream_kernel(a_ref, wc_ref, *rest, epilogue, w_transposed, out_transposed, ni, nc):
    wb_ref = rest[-1]
    o_ref = rest[-2]
    s = pl.program_id(0)
    rows = wc_ref.shape[0]
    u = jnp.maximum(s - nc, 0)
    j = u // ni
    chunk = jnp.minimum(u % ni, nc - 1)

    def cast_chunk(slot, c):
        r0 = pl.multiple_of(c * rows, rows)
        wb_ref[slot, pl.ds(r0, rows), :] = wc_ref[...].astype(BF16)

    @pl.when(s < nc)
    def _():
        cast_chunk(0, s)

    @pl.when(s >= nc)
    def _():
        wb = wb_ref[j % 2]
        nt_dims = (((1,), (1,)), ((), ()))
        if out_transposed:
            y = lax.dot_general(wb, a_ref[...], nt_dims, preferred_element_type=F32)
        elif w_transposed:
            y = lax.dot_general(a_ref[...], wb, nt_dims, preferred_element_type=F32)
        else:
            y = jnp.dot(a_ref[...], wb, preferred_element_type=F32)
        _epilogue(y, rest[:-2], o_ref, epilogue)
        cast_chunk((j + 1) % 2, chunk)


def _proj_stream(a, w, *, n, col0, out_dtype, epilogue, tm, tn, w_transposed=False,
                 out_transposed=False, row_vecs=(), tiles=(), nc=W_CHUNKS, vmem_mib=56, name):
    m, k = a.shape
    nj, ni = n // tn, m // tm
    assert n % tn == 0 and m % tm == 0 and ni >= nc
    assert not out_transposed or (w_transposed and not row_vecs and not tiles)

    where_to = functools.partial(_stream_step, ni=ni, nj=nj, nc=nc)

    if w_transposed:
        assert col0 % SUBLANES == 0 and tn % nc == 0
        cr = tn // nc

        def w_index(s):
            tile, chunk, _, _ = where_to(s)
            return pl.multiple_of(col0 + tile * tn + chunk * cr, SUBLANES), 0

        w_spec = pl.BlockSpec((pl.Element(cr), pl.Element(k)), w_index)
        wb_shape = (2, tn, k)
    else:
        assert col0 % tn == 0 and k % nc == 0
        ck = k // nc

        def w_index(s):
            tile, chunk, _, _ = where_to(s)
            return chunk, col0 // tn + tile

        w_spec = pl.BlockSpec((ck, tn), w_index)
        wb_shape = (2, k, tn)

    def a_index(s):
        _, _, _, i = where_to(s)
        return i, 0

    def vec_index(s):
        _, _, j, _ = where_to(s)
        return 0, j

    def out_index(s):
        _, _, j, i = where_to(s)
        return (j, i) if out_transposed else (i, j)

    in_specs = [pl.BlockSpec((tm, k), a_index), w_spec]
    args = [a, w]
    for t in tiles:
        in_specs.append(pl.BlockSpec((tm, tn), out_index))
        args.append(t)
    for v in row_vecs:
        in_specs.append(pl.BlockSpec((1, tn), vec_index))
        args.append(v.reshape(1, n))
    if out_transposed:
        out_spec = pl.BlockSpec((tn, tm), out_index)
        out_shape = jax.ShapeDtypeStruct((n, m), out_dtype)
    else:
        out_spec = pl.BlockSpec((tm, tn), out_index)
        out_shape = jax.ShapeDtypeStruct((m, n), out_dtype)
    return pl.pallas_call(
        functools.partial(_proj_stream_kernel, epilogue=epilogue, w_transposed=w_transposed,
                          out_transposed=out_transposed, ni=ni, nc=nc),
        grid=(nc + nj * ni,),
        in_specs=in_specs,
        out_specs=out_spec,
        out_shape=out_shape,
        scratch_shapes=[pltpu.VMEM(wb_shape, BF16)],
        compiler_params=_params(("arbitrary",), vmem_mib),
        name=name,
    )(*args)


def _cumsum_kernel(x_ref, ft_ref, *, chunk):
    s = x_ref.shape[0]
    row = lax.broadcasted_iota(jnp.int32, (chunk, LANES), 0)

    def body(c, carry):
        s0 = pl.multiple_of(c * chunk, chunk)
        x = x_ref[pl.ds(s0, chunk), :]
        d = 1
        while d < chunk:
            x = x + jnp.where(row >= d, pltpu.roll(x, d, 0), 0.0)
            d *= 2
        x = x + carry
        ft_ref[:, pl.ds(s0, chunk)] = x.T
        return x[chunk - 1:chunk, :]

    lax.fori_loop(0, s // chunk, body, jnp.zeros((1, LANES), F32))


def _cumsum_t(logf, *, chunk=512):
    s = logf.shape[0]
    return pl.pallas_call(
        functools.partial(_cumsum_kernel, chunk=chunk),
        out_shape=jax.ShapeDtypeStruct((LANES, s), F32),
        compiler_params=_params(None, 40),
        name="cumsum",
    )(logf)


N_SPLIT = 3


def _decay_features(ft_ref, qf_ref, kf_ref, *, chunk=4096):
    s = ft_ref.shape[1]
    row = lax.broadcasted_iota(jnp.int32, (LANES, chunk), 0)

    def body(c, carry):
        c0 = pl.multiple_of(c * chunk, chunk)
        rem = ft_ref[:, pl.ds(c0, chunk)] * LOG2E
        qf = jnp.where((row >= N_SPLIT) & (row < 2 * N_SPLIT), 1.0, 0.0)
        kf = jnp.where(row < N_SPLIT, 1.0, 0.0)
        for p in range(N_SPLIT):
            piece = rem.astype(BF16).astype(F32)
            rem = rem - piece
            qf = jnp.where(row == p, piece, qf)
            kf = jnp.where(row == N_SPLIT + p, -piece, kf)
        qf_ref[pl.ds(c0, chunk), :] = qf.astype(BF16).T
        kf_ref[pl.ds(c0, chunk), :] = kf.astype(BF16).T
        return carry

    lax.fori_loop(0, s // chunk, body, 0)


VA_ROWS = HEAD_DIM + 16
QUERY_CHUNK = 256


def _attn_kernel(q_ref, k_ref, vt_ref, ft_ref, z_ref, o_ref, qf_ref, kf_ref, vat_ref, qa_ref, sa_ref,
                 sb_ref, m_ref, acc_ref, *, tq, tk):
    i = pl.program_id(1)
    nt_dims = (((1,), (1,)), ((), ()))

    @pl.when(i == 0)
    def _():
        _decay_features(ft_ref, qf_ref, kf_ref)
        vat_ref[:HEAD_DIM, :] = vt_ref[...]
        row = lax.broadcasted_iota(jnp.int32, (VA_ROWS - HEAD_DIM, vat_ref.shape[1]), 0)
        vat_ref[HEAD_DIM:, :] = jnp.where(row == 0, 1.0, 0.0).astype(BF16)

    q0 = pl.multiple_of(i * tq, tq)
    qa_ref[:, :HEAD_DIM] = q_ref[...]
    qa_ref[:, HEAD_DIM:] = qf_ref[pl.ds(q0, tq), :]

    m_ref[...] = jnp.full(m_ref.shape, MASK_VALUE, F32)
    acc_ref[...] = jnp.zeros(acc_ref.shape, F32)

    def scores(j, dst_ref, c0=0):
        k0 = pl.multiple_of(j * tk, tk)
        ka = jnp.concatenate([k_ref[pl.ds(k0, tk), :], kf_ref[pl.ds(k0, tk), :]], axis=1)
        dst_ref[:, c0:] = lax.dot_general(ka, qa_ref[c0:, :], nt_dims,
                                          preferred_element_type=F32)

    def consume(src_ref, j, c0=0, causal=False):
        k0 = pl.multiple_of(j * tk, tk)
        row0 = pl.multiple_of(jnp.minimum(i, 0) * SUBLANES, SUBLANES)
        for q1 in range(c0, tq, QUERY_CHUNK):
            qs = slice(q1, q1 + QUERY_CHUNK)
            s2 = src_ref[pl.ds(row0, tk), qs]
            if causal and q1 - c0 < tk:
                keys = lax.broadcasted_iota(jnp.int32, (tk, QUERY_CHUNK), 0)
                qrys = lax.broadcasted_iota(jnp.int32, (tk, QUERY_CHUNK), 1) + (q1 - c0)
                s2 = jnp.where(qrys >= keys, s2, MASK_VALUE)
            m_prev = m_ref[:, qs]
            m_new = jnp.maximum(m_prev, jnp.max(s2, axis=0, keepdims=True))
            alpha = jnp.exp2(m_prev - m_new)
            p = jnp.exp2((s2 - m_new).astype(BF16))
            pv = jnp.dot(vat_ref[:, pl.ds(k0, tk)], p, preferred_element_type=F32)
            acc_ref[:, qs] = alpha * acc_ref[:, qs] + pv
            m_ref[:, qs] = m_new

    nk = tq // tk
    scores(0, sa_ref)

    def pair(jj):
        j = 2 * jj
        scores(j + 1, sb_ref)
        consume(sa_ref, j)
        scores(j + 2, sa_ref)
        consume(sb_ref, j + 1)

    def quad(qq, carry):
        pair(2 * qq)
        pair(2 * qq + 1)
        return carry

    n_pairs = i * (nk // 2)
    lax.fori_loop(0, n_pairs // 2, quad, 0)

    @pl.when(n_pairs % 2 == 1)
    def _():
        pair(n_pairs - 1)

    bufs = (sa_ref, sb_ref)
    for d in range(nk):
        if d + 1 < nk:
            scores(nk * i + d + 1, bufs[(d + 1) % 2], c0=(d + 1) * tk)
        consume(bufs[d % 2], nk * i + d, c0=d * tk, causal=True)

    acc = acc_ref[...]
    att = (acc[:HEAD_DIM, :] / acc[HEAD_DIM:HEAD_DIM + 1, :]).T
    o_ref[...] = (att * z_ref[...].astype(F32)).astype(o_ref.dtype)


def _attention(qk, vt, z, ft, *, heads, tq=1024, tk=512):
    s = qk.shape[0]
    assert tq % (2 * tk) == 0 and s % tq == 0
    return pl.pallas_call(
        functools.partial(_attn_kernel, tq=tq, tk=tk),
        grid=(heads, s // tq),
        in_specs=[pl.BlockSpec((tq, HEAD_DIM), lambda h, i: (i, h)),
                  pl.BlockSpec((s, HEAD_DIM), lambda h, i: (0, heads + h)),
                  pl.BlockSpec((HEAD_DIM, s), lambda h, i: (h, 0)),
                  pl.BlockSpec((None, 1, s), lambda h, i: (h, 0, 0)),
                  pl.BlockSpec((tq, HEAD_DIM), lambda h, i: (i, h))],
        out_specs=pl.BlockSpec((tq, HEAD_DIM), lambda h, i: (i, h)),
        out_shape=jax.ShapeDtypeStruct((s, heads * HEAD_DIM), BF16),
        scratch_shapes=[pltpu.VMEM((s, LANES), BF16),
                        pltpu.VMEM((s, LANES), BF16),
                        pltpu.VMEM((VA_ROWS, s), BF16),
                        pltpu.VMEM((tq, HEAD_DIM + LANES), BF16),
                        pltpu.VMEM((tk, tq), F32),
                        pltpu.VMEM((tk, tq), F32),
                        pltpu.VMEM((1, tq), F32),
                        pltpu.VMEM((VA_ROWS, tq), F32)],
        compiler_params=_params(("parallel", "arbitrary"), 48),
        name="attn",
    )(qk, qk, vt, ft, z)


def _rnn_kernel(x_ref, z_ref, cw_ref, cb_ref, wax_ref, ba_ref, bx_ref, lam_ref, o_ref,
                xpad_ref, hc_ref, a_ref, b_ref, *, tt, wt):
    t = pl.program_id(1)
    groups = tt // SUBLANES

    @pl.when(t == 0)
    def _():
        xpad_ref[:SUBLANES, :] = jnp.zeros((SUBLANES, wt), F32)
        hc_ref[...] = jnp.zeros(hc_ref.shape, F32)

    x = x_ref[...]
    xpad_ref[SUBLANES:, :] = x
    xc = cb_ref[...] + x * cw_ref[CONV_W - 1:CONV_W, :]
    for j in range(1, CONV_W):
        xc = xc + xpad_ref[pl.ds(SUBLANES - j, tt), :] * cw_ref[CONV_W - 1 - j:CONV_W - j, :]
    xpad_ref[:SUBLANES, :] = x[tt - SUBLANES:, :]

    xcb = xc.astype(BF16)
    ga, gx = [], []
    for n in range(wt // LANES):
        g = jnp.dot(xcb[:, n * LANES:(n + 1) * LANES], wax_ref[n], preferred_element_type=F32)
        ga.append(g[:, :LANES])
        gx.append(g[:, LANES:])
    r_gate = _sigmoid(jnp.concatenate(ga, axis=1) + ba_ref[...])
    i_gate = _sigmoid(jnp.concatenate(gx, axis=1) + bx_ref[...])
    log_a = (-RG_C) * r_gate * _softplus(-lam_ref[...])
    a = jnp.exp(log_a)
    mult = jnp.sqrt(-jnp.tanh(log_a) * (a * a + 1.0))
    b = mult * (i_gate * xc)

    a3 = a.reshape(groups, SUBLANES, wt)
    b3 = b.reshape(groups, SUBLANES, wt)
    sub = lax.broadcasted_iota(jnp.int32, (groups, SUBLANES, wt), 1)
    d = 1
    while d < SUBLANES:
        keep = sub >= d
        b3 = a3 * jnp.where(keep, pltpu.roll(b3, d, 1), 0.0) + b3
        a3 = a3 * jnp.where(keep, pltpu.roll(a3, d, 1), 1.0)
        d *= 2
    a_ref[...] = a3.reshape(tt, wt)
    b_ref[...] = b3.reshape(tt, wt)

    def body(g, hprev):
        g0 = pl.multiple_of(g * 2 * SUBLANES, 2 * SUBLANES)
        a2 = a_ref[pl.ds(g0, 2 * SUBLANES), :]
        b2 = b_ref[pl.ds(g0, 2 * SUBLANES), :]
        h0 = b2[:SUBLANES] + a2[:SUBLANES] * hprev
        hmid = jnp.broadcast_to(h0[SUBLANES - 1:SUBLANES, :], (SUBLANES, wt))
        h1 = b2[SUBLANES:] + a2[SUBLANES:] * hmid
        h2 = jnp.concatenate([h0, h1], axis=0)
        o_ref[pl.ds(g0, 2 * SUBLANES), :] = (
            h2 * z_ref[pl.ds(g0, 2 * SUBLANES), :].astype(F32)).astype(o_ref.dtype)
        return jnp.broadcast_to(h1[SUBLANES - 1:SUBLANES, :], (SUBLANES, wt))

    hc_ref[...] = lax.fori_loop(0, groups // 2, body, hc_ref[...], unroll=8)


def _rnn(x_r, z_r, conv_w, conv_b, w_ax, b_a, b_x, lam, *, tt=2048, wt=512):
    s, w = x_r.shape
    nb = wt // LANES
    tile = pl.BlockSpec((tt, wt), lambda n, t: (t, n))
    vec = pl.BlockSpec((1, wt), lambda n, t: (0, n))
    return pl.pallas_call(
        functools.partial(_rnn_kernel, tt=tt, wt=wt),
        grid=(w // wt, s // tt),
        in_specs=[tile, tile,
                  pl.BlockSpec((CONV_W, wt), lambda n, t: (0, n)), vec,
                  pl.BlockSpec((nb, LANES, 2 * LANES), lambda n, t: (n, 0, 0)),
                  vec, vec, vec],
        out_specs=tile,
        out_shape=jax.ShapeDtypeStruct((s, w), BF16),
        scratch_shapes=[pltpu.VMEM((SUBLANES + tt, wt), F32),
                        pltpu.VMEM((SUBLANES, wt), F32),
                        pltpu.VMEM((tt, wt), F32),
                        pltpu.VMEM((tt, wt), F32)],
        compiler_params=_params(("parallel", "arbitrary"), 40),
        name="rnn",
    )(x_r, z_r, conv_w, conv_b.reshape(1, w), w_ax, b_a.reshape(1, w), b_x.reshape(1, w),
      lam.reshape(1, w))


def _merge_kernel(ya_ref, yr_ref, ga_ref, gr_ref, wca_ref, wcr_ref, o_ref, wa_ref, wr_ref, *, ni, nc):
    s = pl.program_id(0)
    rows = wca_ref.shape[0]
    u = jnp.maximum(s - nc, 0)
    j = u // ni
    chunk = jnp.minimum(u % ni, nc - 1)

    def cast_chunk(slot, c):
        r0 = pl.multiple_of(c * rows, rows)
        wa_ref[slot, pl.ds(r0, rows), :] = wca_ref[...].astype(BF16)
        wr_ref[slot, pl.ds(r0, rows), :] = wcr_ref[...].astype(BF16)

    @pl.when(s < nc)
    def _():
        cast_chunk(0, s)

    @pl.when(s >= nc)
    def _():
        y_a = jnp.dot(ya_ref[...], wa_ref[j % 2], preferred_element_type=F32)
        acc = ga_ref[...].astype(F32) * y_a
        y_r = jnp.dot(yr_ref[...], wr_ref[j % 2], preferred_element_type=F32)
        o_ref[...] = (acc + gr_ref[...].astype(F32) * y_r).astype(o_ref.dtype)
        cast_chunk((j + 1) % 2, chunk)


def _merge(ya, yr, g, w_ba, w_br, *, tm=1024, tn=1024, nc=8):
    s, wd = ya.shape
    d = w_ba.shape[1]
    nj, ni = d // tn, s // tm
    assert d % tn == 0 and s % tm == 0 and wd % nc == 0 and ni >= nc
    step = functools.partial(_stream_step, ni=ni, nj=nj, nc=nc)
    act = pl.BlockSpec((tm, wd), lambda t: (step(t)[3], 0))
    wchunk = pl.BlockSpec((wd // nc, tn), lambda t: (step(t)[1], step(t)[0]))
    return pl.pallas_call(
        functools.partial(_merge_kernel, ni=ni, nc=nc),
        grid=(nc + nj * ni,),
        in_specs=[act, act,
                  pl.BlockSpec((tm, tn), lambda t: (step(t)[3], step(t)[2])),
                  pl.BlockSpec((tm, tn), lambda t: (step(t)[3], nj + step(t)[2])),
                  wchunk, wchunk],
        out_specs=pl.BlockSpec((tm, tn), lambda t: (step(t)[3], step(t)[2])),
        out_shape=jax.ShapeDtypeStruct((s, d), BF16),
        scratch_shapes=[pltpu.VMEM((2, wd, tn), BF16), pltpu.VMEM((2, wd, tn), BF16)],
        compiler_params=_params(("arbitrary",), 60),
        name="merge",
    )(ya, yr, g, g, w_ba, w_br)


def kernel(x, c, w_ada, b_ada, norm_w, w_in, b_f, q_norm_w, k_norm_w, conv_w, conv_b, w_rg_a,
           b_rg_a, w_rg_x, b_rg_x, lru_lambda, w_br_attn, w_br_rnn, w_gate, b_gate, w_out):
    bsz, s, d = x.shape
    assert bsz == 1
    aw = w_br_attn.shape[0]
    rw = w_br_rnn.shape[0]
    heads = aw // HEAD_DIM
    x2 = x.reshape(s, d)

    mod = _ada(c, w_ada, b_ada)
    shift, scale, gate_res = mod[:, :d], mod[:, d:2 * d], mod[:, 2 * d:]
    o_f = 4 * aw
    o_xr = o_f + heads
    h, logf = _prenorm(x2, norm_w, scale, shift, w_in.T, o_f, b_f)
    qk_gain = jnp.concatenate([jnp.tile(q_norm_w * (LOG2E / math.sqrt(HEAD_DIM)), heads),
                               jnp.tile(k_norm_w, heads)])
    proj = functools.partial(_proj_stream, h, w_in.T, w_transposed=True, tm=1024, tn=1024)
    qk = proj(n=2 * aw, col0=0, out_dtype=BF16, epilogue="headnorm", row_vecs=(qk_gain,),
              name="proj_qk")
    vt = proj(n=aw, col0=2 * aw, out_dtype=BF16, epilogue="plain", out_transposed=True,
              name="proj_v")
    z_a = proj(n=aw, col0=3 * aw, out_dtype=BF16, epilogue="silu", name="proj_za")
    x_r = proj(n=rw, col0=o_xr, out_dtype=F32, epilogue="plain", name="proj_xr")
    z_r = proj(n=rw, col0=o_xr + rw, out_dtype=BF16, epilogue="silu", name="proj_zr")

    ft = _cumsum_t(logf)[:heads].reshape(heads, 1, s)
    ya = _attention(qk, vt, z_a, ft, heads=heads)

    w_ax = jnp.concatenate([w_rg_a, w_rg_x], axis=2).astype(BF16)
    yr = _rnn(x_r, z_r, conv_w, conv_b, w_ax, b_rg_a, b_rg_x, lru_lambda)

    g = _proj_stream(h, w_gate, n=2 * d, col0=0, out_dtype=BF16, epilogue="sigmoid_bias", tm=1024,
                     tn=1024, row_vecs=(b_gate,), name="gate")
    merged = _merge(ya, yr, g, w_br_attn, w_br_rnn)
    y = _proj_stream(merged, w_out, n=d, col0=0, out_dtype=x.dtype, epilogue="residual", tm=1024,
                     tn=1024, tiles=(x2,), row_vecs=(gate_res.reshape(-1),), nc=8, vmem_mib=60,
                     name="out")
    return y.reshape(bsz, s, d)
```

```python
import functools
import math

import jax
import jax.numpy as jnp
from jax import lax
from jax.experimental import pallas as pl
from jax.experimental.pallas import tpu as pltpu

F32 = jnp.float32
BF16 = jnp.bfloat16

LANES = 128
SUBLANES = 8
HEAD_DIM = 128
CONV_W = 4
RG_C = 8.0
EPS = 1e-6
MASK_VALUE = -1e30
LOG2E = 1.4426950408889634
MIB = 1024 * 1024


def _params(semantics, vmem_mib, flags=None):
    return pltpu.CompilerParams(dimension_semantics=semantics,
                                vmem_limit_bytes=vmem_mib * MIB, flags=flags)


def _sigmoid(x):
    return 0.5 * jnp.tanh(0.5 * x) + 0.5


def _silu(x):
    return x * _sigmoid(x)


def _softplus(x):
    return jnp.maximum(x, 0.0) + jnp.log1p(jnp.exp(-jnp.abs(x)))


def _ada_kernel(c_ref, w_ref, b_ref, o_ref, *, rows):
    d = w_ref.shape[0]

    def body(r, acc):
        r0 = pl.multiple_of(r * rows, rows)
        c = c_ref[pl.ds(r0, rows), :]
        return acc + jnp.sum(w_ref[pl.ds(r0, rows), :] * _silu(c), axis=0, keepdims=True)

    acc = lax.fori_loop(0, d // rows, body, jnp.zeros(o_ref.shape, F32))
    o_ref[...] = acc + b_ref[...]


def _ada(c, w_ada, b_ada, *, tn=512, rows=512):
    d, n = w_ada.shape
    return pl.pallas_call(
        functools.partial(_ada_kernel, rows=rows),
        grid=(n // tn,),
        in_specs=[pl.BlockSpec((d, 1), lambda j: (0, 0)),
                  pl.BlockSpec((d, tn), lambda j: (0, j)),
                  pl.BlockSpec((1, tn), lambda j: (0, j))],
        out_specs=pl.BlockSpec((1, tn), lambda j: (0, j)),
        out_shape=jax.ShapeDtypeStruct((1, n), F32),
        compiler_params=_params(("parallel",), 40),
        name="ada",
    )(c.reshape(d, 1), w_ada, b_ada.reshape(1, n))


def _prenorm_kernel(x_ref, nw_ref, scale_ref, shift_ref, wf_ref, bf_ref, h_ref, lf_ref, wfb_ref):
    @pl.when(pl.program_id(0) == 0)
    def _():
        wfb_ref[...] = wf_ref[...].astype(BF16)

    x = x_ref[...]
    ms = jnp.mean(x * x, axis=-1, keepdims=True)
    h = x * lax.rsqrt(ms + EPS) * nw_ref[...]
    h = (h * (1.0 + scale_ref[...]) + shift_ref[...]).astype(h_ref.dtype)
    h_ref[...] = h
    z = lax.dot_general(h, wfb_ref[...], (((1,), (1,)), ((), ())),
                        preferred_element_type=F32) + bf_ref[...]
    lf_ref[...] = jnp.minimum(z, 0.0) - jnp.log1p(jnp.exp(-jnp.abs(z)))


def _prenorm(x, norm_w, scale, shift, w_t, col_f, b_f, *, tm=512):
    s, d = x.shape
    assert col_f % SUBLANES == 0
    vec = pl.BlockSpec((1, d), lambda i: (0, 0))
    b_pad = jnp.pad(b_f, (0, LANES - b_f.shape[0])).reshape(1, LANES)
    return pl.pallas_call(
        _prenorm_kernel,
        grid=(s // tm,),
        in_specs=[pl.BlockSpec((tm, d), lambda i: (i, 0)), vec, vec, vec,
                  pl.BlockSpec((pl.Element(LANES), pl.Element(d)), lambda i: (col_f, 0)),
                  pl.BlockSpec((1, LANES), lambda i: (0, 0))],
        out_specs=(pl.BlockSpec((tm, d), lambda i: (i, 0)),
                   pl.BlockSpec((tm, LANES), lambda i: (i, 0))),
        out_shape=(jax.ShapeDtypeStruct((s, d), BF16), jax.ShapeDtypeStruct((s, LANES), F32)),
        scratch_shapes=[pltpu.VMEM((LANES, d), BF16)],
        compiler_params=_params(("arbitrary",), 40),
        name="prenorm",
    )(x, norm_w.reshape(1, d), scale, shift, w_t, b_pad)


def _epilogue(y, rest, o_ref, epilogue):
    if epilogue == "plain":
        o_ref[...] = y.astype(o_ref.dtype)
    elif epilogue == "silu":
        o_ref[...] = _silu(y).astype(o_ref.dtype)
    elif epilogue == "sigmoid_bias":
        o_ref[...] = _sigmoid(y + rest[0][...]).astype(o_ref.dtype)
    elif epilogue == "headnorm":
        gain_ref = rest[0]
        for hh in range(y.shape[1] // HEAD_DIM):
            sl = slice(hh * HEAD_DIM, (hh + 1) * HEAD_DIM)
            yh = y[:, sl]
            ms = jnp.mean(yh * yh, axis=-1, keepdims=True)
            o_ref[:, sl] = (yh * lax.rsqrt(ms + EPS) * gain_ref[:, sl]).astype(o_ref.dtype)
    elif epilogue == "residual":
        x_ref, g_ref = rest[0], rest[1]
        o_ref[...] = (x_ref[...] + g_ref[...] * y).astype(o_ref.dtype)
    else:
        raise ValueError(epilogue)


W_CHUNKS = 4


def _stream_step(s, *, ni, nj, nc):
    u = jnp.maximum(s - nc, 0)
    j, i = u // ni, u % ni
    pro = s < nc
    last = j == nj - 1
    tile = jnp.where(pro, 0, jnp.minimum(j + 1, nj - 1))
    chunk = jnp.where(pro, s, jnp.where(last, nc - 1, jnp.minimum(i, nc - 1)))
    return tile, chunk, j, i


def _proj_stream_kernel(a_ref, wc_ref, *rest, epilogue, w_transposed, out_transposed, ni, nc):
    wb_ref = rest[-1]
    o_ref = rest[-2]
    s = pl.program_id(0)
    rows = wc_ref.shape[0]
    u = jnp.maximum(s - nc, 0)
    j = u // ni
    chunk = jnp.minimum(u % ni, nc - 1)

    def cast_chunk(slot, c):
        r0 = pl.multiple_of(c * rows, rows)
        wb_ref[slot, pl.ds(r0, rows), :] = wc_ref[...].astype(BF16)

    @pl.when(s < nc)
    def _():
        cast_chunk(0, s)

    @pl.when(s >= nc)
    def _():
        wb = wb_ref[j % 2]
        nt_dims = (((1,), (1,)), ((), ()))
        if out_transposed:
            y = lax.dot_general(wb, a_ref[...], nt_dims, preferred_element_type=F32)
        elif w_transposed:
            y = lax.dot_general(a_ref[...], wb, nt_dims, preferred_element_type=F32)
        else:
            y = jnp.dot(a_ref[...], wb, preferred_element_type=F32)
        _epilogue(y, rest[:-2], o_ref, epilogue)
        cast_chunk((j + 1) % 2, chunk)


def _proj_stream(a, w, *, n, col0, out_dtype, epilogue, tm, tn, w_transposed=False,
                 out_transposed=False, row_vecs=(), tiles=(), nc=W_CHUNKS, vmem_mib=56,
                 tile_cols=None, name):
    m, k = a.shape
    nj, ni = n // tn, m // tm
    assert n % tn == 0 and m % tm == 0 and ni >= nc
    assert not out_transposed or (w_transposed and not row_vecs and not tiles)

    where_to = functools.partial(_stream_step, ni=ni, nj=nj, nc=nc)

    if w_transposed:
        assert col0 % SUBLANES == 0 and tn % nc == 0
        cr = tn // nc

        starts = tile_cols if tile_cols is not None else [col0 + t * tn for t in range(nj)]
        assert len(starts) == nj and all(c % SUBLANES == 0 for c in starts)

        def w_index(s):
            tile, chunk, _, _ = where_to(s)
            start = starts[0]
            for t in range(1, nj):
                start = start + jnp.where(tile >= t, starts[t] - starts[t - 1], 0)
            return pl.multiple_of(start + chunk * cr, SUBLANES), 0

        w_spec = pl.BlockSpec((pl.Element(cr), pl.Element(k)), w_index)
        wb_shape = (2, tn, k)
    else:
        assert col0 % tn == 0 and k % nc == 0 and tile_cols is None
        ck = k // nc

        def w_index(s):
            tile, chunk, _, _ = where_to(s)
            return chunk, col0 // tn + tile

        w_spec = pl.BlockSpec((ck, tn), w_index)
        wb_shape = (2, k, tn)

    def a_index(s):
        _, _, _, i = where_to(s)
        return i, 0

    def vec_index(s):
        _, _, j, _ = where_to(s)
        return 0, j

    def out_index(s):
        _, _, j, i = where_to(s)
        return (j, i) if out_transposed else (i, j)

    in_specs = [pl.BlockSpec((tm, k), a_index), w_spec]
    args = [a, w]
    for t in tiles:
        in_specs.append(pl.BlockSpec((tm, tn), out_index))
        args.append(t)
    for v in row_vecs:
        in_specs.append(pl.BlockSpec((1, tn), vec_index))
        args.append(v.reshape(1, n))
    if out_transposed:
        out_spec = pl.BlockSpec((tn, tm), out_index)
        out_shape = jax.ShapeDtypeStruct((n, m), out_dtype)
    else:
        out_spec = pl.BlockSpec((tm, tn), out_index)
        out_shape = jax.ShapeDtypeStruct((m, n), out_dtype)
    return pl.pallas_call(
        functools.partial(_proj_stream_kernel, epilogue=epilogue, w_transposed=w_transposed,
                          out_transposed=out_transposed, ni=ni, nc=nc),
        grid=(nc + nj * ni,),
        in_specs=in_specs,
        out_specs=out_spec,
        out_shape=out_shape,
        scratch_shapes=[pltpu.VMEM(wb_shape, BF16)],
        compiler_params=_params(("arbitrary",), vmem_mib),
        name=name,
    )(*args)


def _cumsum_kernel(x_ref, ft_ref, *, chunk):
    s = x_ref.shape[0]
    row = lax.broadcasted_iota(jnp.int32, (chunk, LANES), 0)

    def body(c, carry):
        s0 = pl.multiple_of(c * chunk, chunk)
        x = x_ref[pl.ds(s0, chunk), :]
        d = 1
        while d < chunk:
            x = x + jnp.where(row >= d, pltpu.roll(x, d, 0), 0.0)
            d *= 2
        x = x + carry
        ft_ref[:, pl.ds(s0, chunk)] = x.T
        return x[chunk - 1:chunk, :]

    lax.fori_loop(0, s // chunk, body, jnp.zeros((1, LANES), F32))


def _cumsum_t(logf, *, chunk=512):
    s = logf.shape[0]
    return pl.pallas_call(
        functools.partial(_cumsum_kernel, chunk=chunk),
        out_shape=jax.ShapeDtypeStruct((LANES, s), F32),
        compiler_params=_params(None, 40),
        name="cumsum",
    )(logf)


N_SPLIT = 3


def _decay_features(ft_ref, qf_ref, kf_ref, *, chunk=4096):
    s = ft_ref.shape[1]
    row = lax.broadcasted_iota(jnp.int32, (LANES, chunk), 0)

    def body(c, carry):
        c0 = pl.multiple_of(c * chunk, chunk)
        rem = ft_ref[:, pl.ds(c0, chunk)] * LOG2E
        qf = jnp.where((row >= N_SPLIT) & (row < 2 * N_SPLIT), 1.0, 0.0)
        kf = jnp.where(row < N_SPLIT, 1.0, 0.0)
        for p in range(N_SPLIT):
            piece = rem.astype(BF16).astype(F32)
            rem = rem - piece
            qf = jnp.where(row == p, piece, qf)
            kf = jnp.where(row == N_SPLIT + p, -piece, kf)
        qf_ref[pl.ds(c0, chunk), :] = qf.astype(BF16).T
        kf_ref[pl.ds(c0, chunk), :] = kf.astype(BF16).T
        return carry

    lax.fori_loop(0, s // chunk, body, 0)


VA_ROWS = HEAD_DIM + 16
QUERY_CHUNK = 256


def _attn_kernel(q_ref, k_ref, vt_ref, ft_ref, z_ref, o_ref, qf_ref, kf_ref, vat_ref, qa_ref, sa_ref,
                 sb_ref, m_ref, acc_ref, *, tq, tk):
    i = pl.program_id(1)
    nt_dims = (((1,), (1,)), ((), ()))

    @pl.when(i == 0)
    def _():
        _decay_features(ft_ref, qf_ref, kf_ref)
        vat_ref[:HEAD_DIM, :] = vt_ref[...]
        row = lax.broadcasted_iota(jnp.int32, (VA_ROWS - HEAD_DIM, vat_ref.shape[1]), 0)
        vat_ref[HEAD_DIM:, :] = jnp.where(row == 0, 1.0, 0.0).astype(BF16)

    q0 = pl.multiple_of(i * tq, tq)
    qa_ref[:, :HEAD_DIM] = q_ref[...]
    qa_ref[:, HEAD_DIM:] = qf_ref[pl.ds(q0, tq), :]

    m_ref[...] = jnp.full(m_ref.shape, MASK_VALUE, F32)
    acc_ref[...] = jnp.zeros(acc_ref.shape, F32)

    def scores(j, dst_ref, c0=0):
        k0 = pl.multiple_of(j * tk, tk)
        ka = jnp.concatenate([k_ref[pl.ds(k0, tk), :], kf_ref[pl.ds(k0, tk), :]], axis=1)
        dst_ref[:, c0:] = lax.dot_general(ka, qa_ref[c0:, :], nt_dims,
                                          preferred_element_type=F32)

    def consume(src_ref, j, c0=0, causal=False):
        k0 = pl.multiple_of(j * tk, tk)
        row0 = pl.multiple_of(jnp.minimum(i, 0) * SUBLANES, SUBLANES)
        for q1 in range(c0, tq, QUERY_CHUNK):
            qs = slice(q1, q1 + QUERY_CHUNK)
            s2 = src_ref[pl.ds(row0, tk), qs]
            if causal and q1 - c0 < tk:
                keys = lax.broadcasted_iota(jnp.int32, (tk, QUERY_CHUNK), 0)
                qrys = lax.broadcasted_iota(jnp.int32, (tk, QUERY_CHUNK), 1) + (q1 - c0)
                s2 = jnp.where(qrys >= keys, s2, MASK_VALUE)
            m_prev = m_ref[:, qs]
            m_new = jnp.maximum(m_prev, jnp.max(s2, axis=0, keepdims=True))
            alpha = jnp.exp2(m_prev - m_new)
            p = jnp.exp2((s2 - m_new).astype(BF16))
            pv = jnp.dot(vat_ref[:, pl.ds(k0, tk)], p, preferred_element_type=F32)
            acc_ref[:, qs] = alpha * acc_ref[:, qs] + pv
            m_ref[:, qs] = m_new

    nk = tq // tk
    scores(0, sa_ref)

    def pair(jj):
        j = 2 * jj
        scores(j + 1, sb_ref)
        consume(sa_ref, j)
        scores(j + 2, sa_ref)
        consume(sb_ref, j + 1)

    def quad(qq, carry):
        pair(2 * qq)
        pair(2 * qq + 1)
        return carry

    n_pairs = i * (nk // 2)
    lax.fori_loop(0, n_pairs // 2, quad, 0)

    @pl.when(n_pairs % 2 == 1)
    def _():
        pair(n_pairs - 1)

    bufs = (sa_ref, sb_ref)
    for d in range(nk):
        if d + 1 < nk:
            scores(nk * i + d + 1, bufs[(d + 1) % 2], c0=(d + 1) * tk)
        consume(bufs[d % 2], nk * i + d, c0=d * tk, causal=True)

    acc = acc_ref[...]
    att = (acc[:HEAD_DIM, :] / acc[HEAD_DIM:HEAD_DIM + 1, :]).T
    o_ref[...] = (att * z_ref[...].astype(F32)).astype(o_ref.dtype)


def _attention(qk, vt, z, ft, *, heads, tq=1024, tk=512):
    s = qk.shape[0]
    assert tq % (2 * tk) == 0 and s % tq == 0
    return pl.pallas_call(
        functools.partial(_attn_kernel, tq=tq, tk=tk),
        grid=(heads, s // tq),
        in_specs=[pl.BlockSpec((tq, HEAD_DIM), lambda h, i: (i, h)),
                  pl.BlockSpec((s, HEAD_DIM), lambda h, i: (0, heads + h)),
                  pl.BlockSpec((HEAD_DIM, s), lambda h, i: (h, 0)),
                  pl.BlockSpec((None, 1, s), lambda h, i: (h, 0, 0)),
                  pl.BlockSpec((tq, HEAD_DIM), lambda h, i: (i, h))],
        out_specs=pl.BlockSpec((tq, HEAD_DIM), lambda h, i: (i, h)),
        out_shape=jax.ShapeDtypeStruct((s, heads * HEAD_DIM), BF16),
        scratch_shapes=[pltpu.VMEM((s, LANES), BF16),
                        pltpu.VMEM((s, LANES), BF16),
                        pltpu.VMEM((VA_ROWS, s), BF16),
                        pltpu.VMEM((tq, HEAD_DIM + LANES), BF16),
                        pltpu.VMEM((tk, tq), F32),
                        pltpu.VMEM((tk, tq), F32),
                        pltpu.VMEM((1, tq), F32),
                        pltpu.VMEM((VA_ROWS, tq), F32)],
        compiler_params=_params(("parallel", "arbitrary"), 48),
        name="attn",
    )(qk, qk, vt, ft, z)


def _rnn_kernel(x_ref, z_ref, cw_ref, cb_ref, wax_ref, ba_ref, bx_ref, lam_ref, o_ref,
                xpad_ref, hc_ref, a_ref, b_ref, *, tt, wt):
    t = pl.program_id(1)
    groups = tt // SUBLANES

    @pl.when(t == 0)
    def _():
        xpad_ref[:SUBLANES, :] = jnp.zeros((SUBLANES, wt), F32)
        hc_ref[...] = jnp.zeros(hc_ref.shape, F32)

    x = x_ref[...]
    xpad_ref[SUBLANES:, :] = x
    xc = cb_ref[...] + x * cw_ref[CONV_W - 1:CONV_W, :]
    for j in range(1, CONV_W):
        xc = xc + xpad_ref[pl.ds(SUBLANES - j, tt), :] * cw_ref[CONV_W - 1 - j:CONV_W - j, :]
    xpad_ref[:SUBLANES, :] = x[tt - SUBLANES:, :]

    xcb = xc.astype(BF16)
    ga, gx = [], []
    for n in range(wt // LANES):
        g = jnp.dot(xcb[:, n * LANES:(n + 1) * LANES], wax_ref[n], preferred_element_type=F32)
        ga.append(g[:, :LANES])
        gx.append(g[:, LANES:])
    r_gate = _sigmoid(jnp.concatenate(ga, axis=1) + ba_ref[...])
    i_gate = _sigmoid(jnp.concatenate(gx, axis=1) + bx_ref[...])
    log_a = (-RG_C) * r_gate * _softplus(-lam_ref[...])
    a = jnp.exp(log_a)
    mult = jnp.sqrt(-jnp.tanh(log_a) * (a * a + 1.0))
    b = mult * (i_gate * xc)

    a3 = a.reshape(groups, SUBLANES, wt)
    b3 = b.reshape(groups, SUBLANES, wt)
    sub = lax.broadcasted_iota(jnp.int32, (groups, SUBLANES, wt), 1)
    d = 1
    while d < SUBLANES:
        keep = sub >= d
        b3 = a3 * jnp.where(keep, pltpu.roll(b3, d, 1), 0.0) + b3
        a3 = a3 * jnp.where(keep, pltpu.roll(a3, d, 1), 1.0)
        d *= 2
    a_ref[...] = a3.reshape(tt, wt)
    b_ref[...] = b3.reshape(tt, wt)

    def body(g, hprev):
        g0 = pl.multiple_of(g * 2 * SUBLANES, 2 * SUBLANES)
        a2 = a_ref[pl.ds(g0, 2 * SUBLANES), :]
        b2 = b_ref[pl.ds(g0, 2 * SUBLANES), :]
        h0 = b2[:SUBLANES] + a2[:SUBLANES] * hprev
        hmid = jnp.broadcast_to(h0[SUBLANES - 1:SUBLANES, :], (SUBLANES, wt))
        h1 = b2[SUBLANES:] + a2[SUBLANES:] * hmid
        h2 = jnp.concatenate([h0, h1], axis=0)
        o_ref[pl.ds(g0, 2 * SUBLANES), :] = (
            h2 * z_ref[pl.ds(g0, 2 * SUBLANES), :].astype(F32)).astype(o_ref.dtype)
        return jnp.broadcast_to(h1[SUBLANES - 1:SUBLANES, :], (SUBLANES, wt))

    hc_ref[...] = lax.fori_loop(0, groups // 2, body, hc_ref[...], unroll=8)


def _rnn(x_r, z_r, conv_w, conv_b, w_ax, b_a, b_x, lam, *, z_col0=0, tt=1024, wt=512):
    s, w = x_r.shape
    nb = wt // LANES
    assert z_col0 % wt == 0
    tile = pl.BlockSpec((tt, wt), lambda n, t: (t, n))
    z_tile = pl.BlockSpec((tt, wt), lambda n, t: (t, z_col0 // wt + n))
    vec = pl.BlockSpec((1, wt), lambda n, t: (0, n))
    return pl.pallas_call(
        functools.partial(_rnn_kernel, tt=tt, wt=wt),
        grid=(w // wt, s // tt),
        in_specs=[tile, z_tile,
                  pl.BlockSpec((CONV_W, wt), lambda n, t: (0, n)), vec,
                  pl.BlockSpec((nb, LANES, 2 * LANES), lambda n, t: (n, 0, 0)),
                  vec, vec, vec],
        out_specs=tile,
        out_shape=jax.ShapeDtypeStruct((s, w), BF16),
        scratch_shapes=[pltpu.VMEM((SUBLANES + tt, wt), F32),
                        pltpu.VMEM((SUBLANES, wt), F32),
                        pltpu.VMEM((tt, wt), F32),
                        pltpu.VMEM((tt, wt), F32)],
        compiler_params=_params(("parallel", "arbitrary"), 40),
        name="rnn",
    )(x_r, z_r, conv_w, conv_b.reshape(1, w), w_ax, b_a.reshape(1, w), b_x.reshape(1, w),
      lam.reshape(1, w))


def _merge_kernel(ya_ref, yr_ref, ga_ref, gr_ref, wca_ref, wcr_ref, o_ref, wa_ref, wr_ref, *, ni, nc):
    s = pl.program_id(0)
    rows = wca_ref.shape[0]
    u = jnp.maximum(s - nc, 0)
    j = u // ni
    chunk = jnp.minimum(u % ni, nc - 1)

    def cast_chunk(slot, c):
        r0 = pl.multiple_of(c * rows, rows)
        wa_ref[slot, pl.ds(r0, rows), :] = wca_ref[...].astype(BF16)
        wr_ref[slot, pl.ds(r0, rows), :] = wcr_ref[...].astype(BF16)

    @pl.when(s < nc)
    def _():
        cast_chunk(0, s)

    @pl.when(s >= nc)
    def _():
        y_a = jnp.dot(ya_ref[...], wa_ref[j % 2], preferred_element_type=F32)
        acc = ga_ref[...].astype(F32) * y_a
        y_r = jnp.dot(yr_ref[...], wr_ref[j % 2], preferred_element_type=F32)
        o_ref[...] = (acc + gr_ref[...].astype(F32) * y_r).astype(o_ref.dtype)
        cast_chunk((j + 1) % 2, chunk)


def _merge(ya, yr, g, w_ba, w_br, *, tm=1024, tn=1024, nc=8):
    s, wd = ya.shape
    d = w_ba.shape[1]
    nj, ni = d // tn, s // tm
    assert d % tn == 0 and s % tm == 0 and wd % nc == 0 and ni >= nc
    step = functools.partial(_stream_step, ni=ni, nj=nj, nc=nc)
    act = pl.BlockSpec((tm, wd), lambda t: (step(t)[3], 0))
    wchunk = pl.BlockSpec((wd // nc, tn), lambda t: (step(t)[1], step(t)[0]))
    return pl.pallas_call(
        functools.partial(_merge_kernel, ni=ni, nc=nc),
        grid=(nc + nj * ni,),
        in_specs=[act, act,
                  pl.BlockSpec((tm, tn), lambda t: (step(t)[3], step(t)[2])),
                  pl.BlockSpec((tm, tn), lambda t: (step(t)[3], nj + step(t)[2])),
                  wchunk, wchunk],
        out_specs=pl.BlockSpec((tm, tn), lambda t: (step(t)[3], step(t)[2])),
        out_shape=jax.ShapeDtypeStruct((s, d), BF16),
        scratch_shapes=[pltpu.VMEM((2, wd, tn), BF16), pltpu.VMEM((2, wd, tn), BF16)],
        compiler_params=_params(("arbitrary",), 60),
        name="merge",
    )(ya, yr, g, g, w_ba, w_br)


def kernel(x, c, w_ada, b_ada, norm_w, w_in, b_f, q_norm_w, k_norm_w, conv_w, conv_b, w_rg_a,
           b_rg_a, w_rg_x, b_rg_x, lru_lambda, w_br_attn, w_br_rnn, w_gate, b_gate, w_out):
    bsz, s, d = x.shape
    assert bsz == 1
    aw = w_br_attn.shape[0]
    rw = w_br_rnn.shape[0]
    heads = aw // HEAD_DIM
    x2 = x.reshape(s, d)

    mod = _ada(c, w_ada, b_ada)
    shift, scale, gate_res = mod[:, :d], mod[:, d:2 * d], mod[:, 2 * d:]
    o_f = 4 * aw
    o_xr = o_f + heads
    h, logf = _prenorm(x2, norm_w, scale, shift, w_in.T, o_f, b_f)
    qk_gain = jnp.concatenate([jnp.tile(q_norm_w * (LOG2E / math.sqrt(HEAD_DIM)), heads),
                               jnp.tile(k_norm_w, heads)])
    proj = functools.partial(_proj_stream, h, w_in.T, w_transposed=True, tm=1024, tn=1024)
    qk = proj(n=2 * aw, col0=0, out_dtype=BF16, epilogue="headnorm", row_vecs=(qk_gain,),
              name="proj_qk")
    vt = proj(n=aw, col0=2 * aw, out_dtype=BF16, epilogue="plain", out_transposed=True,
              name="proj_v")
    z_cols = [3 * aw + t * 1024 for t in range(aw // 1024)] + \
             [o_xr + rw + t * 1024 for t in range(rw // 1024)]
    z_all = proj(n=aw + rw, col0=3 * aw, tile_cols=z_cols, out_dtype=BF16, epilogue="silu",
                 name="proj_z")
    x_r = proj(n=rw, col0=o_xr, out_dtype=F32, epilogue="plain", name="proj_xr")

    ft = _cumsum_t(logf)[:heads].reshape(heads, 1, s)
    ya = _attention(qk, vt, z_all, ft, heads=heads)

    w_ax = jnp.concatenate([w_rg_a, w_rg_x], axis=2).astype(BF16)
    yr = _rnn(x_r, z_all, conv_w, conv_b, w_ax, b_rg_a, b_rg_x, lru_lambda,
              z_col0=aw)

    g = _proj_stream(h, w_gate, n=2 * d, col0=0, out_dtype=BF16, epilogue="sigmoid_bias", tm=1024,
                     tn=1024, row_vecs=(b_gate,), name="gate")
    merged = _merge(ya, yr, g, w_br_attn, w_br_rnn)
    y = _proj_stream(merged, w_out, n=d, col0=0, out_dtype=x.dtype, epilogue="residual", tm=1024,
                     tn=1024, tiles=(x2,), row_vecs=(gate_res.reshape(-1),), nc=8, vmem_mib=60,
                     name="out")
    return y.reshape(bsz, s, d)
```
